```python
import math
import jax, jax.numpy as jnp
from jax import lax
import numpy as np


D_MODEL = 1024
BATCH = 4
SEQ = 8192
DEPTH = 1
DEC_BATCH = 16
DEC_SEQ = 32
PAST_LEN = 2048

CHUNK = 64
LEFT_CHUNKS = 8
WINDOW = CHUNK * LEFT_CHUNKS
BAND = WINDOW + CHUNK
D_MIX = D_MODEL
D_ATTN = D_MIX // 2
D_SSM = D_MIX - D_ATTN
ATTN_HEAD_DIM = 64
ATTN_HEADS = D_ATTN // ATTN_HEAD_DIM
ATTN_SCALE = ATTN_HEAD_DIM ** -0.5
MAX_REL = 128
N_REL = 2 * MAX_REL + 1
NEG_INF = -1e30
SSM_HEAD_DIM = 64
SSM_HEADS = D_SSM // SSM_HEAD_DIM
SSM_GROUPS = 2
SSM_HEADS_PER_GROUP = SSM_HEADS // SSM_GROUPS
SSM_STATE = 128
SSM_CONV = 4
SSM_CONV_DIM = D_SSM + 2 * SSM_GROUPS * SSM_STATE
D_IN_PROJ = 3 * D_ATTN + D_SSM + SSM_CONV_DIM + SSM_HEADS
SPLITS = (D_ATTN, 2 * D_ATTN, 3 * D_ATTN, 3 * D_ATTN + D_SSM, 3 * D_ATTN + D_SSM + SSM_CONV_DIM)
D_FF = 2688
FFN_CONV = 3
ALPHA = (2.0 * DEPTH) ** 0.25
BETA_INIT = (8.0 * DEPTH) ** -0.25
LN_EPS = 1e-5
RMS_EPS = 1e-5

kernel_name = 'hybrid_chunk_attn_ssd_convffn_step'


def layer_norm(x, g, b):
    xf = x.astype(jnp.float32)
    mu = jnp.mean(xf, axis=-1, keepdims=True)
    var = jnp.mean(jnp.square(xf - mu), axis=-1, keepdims=True)
    return ((xf - mu) * lax.rsqrt(var + LN_EPS) * g.astype(jnp.float32) + b.astype(jnp.float32)).astype(x.dtype)


def rms_norm(x, g):
    xf = x.astype(jnp.float32)
    return (xf * lax.rsqrt(jnp.mean(jnp.square(xf), axis=-1, keepdims=True) + RMS_EPS) * g.astype(jnp.float32)).astype(x.dtype)


def causal_dwconv(x, past, w, b):
    K = w.shape[0]
    L = x.shape[1]
    xp = jnp.concatenate([past.astype(x.dtype), x], axis=1)
    out = b + w[0] * xp[:, 0:L]
    for t in range(1, K):
        out = out + w[t] * xp[:, t:t + L]
    return out, xp[:, L:]


def band_attention(q, k, v, rel, valid, rel_bias):
    idx = jnp.clip(rel, -MAX_REL, MAX_REL) + MAX_REL
    bias = rel_bias[:, idx].astype(jnp.float32)
    s = jnp.einsum('bqhd,bkhd->bhqk', q, k).astype(jnp.float32) * ATTN_SCALE + bias
    if valid is not None:
        s = jnp.where(valid, s, NEG_INF)
    p = jax.nn.softmax(s, axis=-1).astype(v.dtype)
    return jnp.einsum('bhqk,bkhd->bqhd', p, v)


def attn_prompt(q, k, v, rel_bias):
    L = q.shape[1]
    n_chunks = L // CHUNK
    pad = ((0, 0), (WINDOW, 0), (0, 0), (0, 0))
    k_pad = jnp.pad(k, pad)
    v_pad = jnp.pad(v, pad)
    rel = (WINDOW + jnp.arange(CHUNK)[:, None]) - jnp.arange(BAND)[None, :]

    def one_chunk(c):
        start = c * CHUNK
        q_c = lax.dynamic_slice_in_dim(q, start, CHUNK, axis=1)
        k_c = lax.dynamic_slice_in_dim(k_pad, start, BAND, axis=1)
        v_c = lax.dynamic_slice_in_dim(v_pad, start, BAND, axis=1)
        valid = (start - WINDOW + jnp.arange(BAND))[None, :] >= 0
        return band_attention(q_c, k_c, v_c, rel, valid, rel_bias)

    out = lax.map(one_chunk, jnp.arange(n_chunks))
    return jnp.moveaxis(out, 0, 1).reshape(q.shape)


def attn_sample(q, k, v, k_past, v_past, rel_bias):
    n_past = k_past.shape[1]
    Lq = q.shape[1]
    kk = jnp.concatenate([k_past.astype(k.dtype), k], axis=1)
    vv = jnp.concatenate([v_past.astype(v.dtype), v], axis=1)
    rel = (n_past + jnp.arange(Lq)[:, None]) - jnp.arange(n_past + Lq)[None, :]
    return band_attention(q, kk, vv, rel, None, rel_bias)


def ssd_scan(x, dt, A, Bm, Cm, h0):
    f32 = jnp.float32
    bsz, L = x.shape[:2]
    G, R, P, N = SSM_GROUPS, SSM_HEADS_PER_GROUP, SSM_HEAD_DIM, SSM_STATE
    n_chunks = -(-L // CHUNK)
    pad_len = n_chunks * CHUNK - L

    def to_chunks(t):
        t = jnp.pad(t, [(0, 0), (0, pad_len)] + [(0, 0)] * (t.ndim - 2))
        return t.reshape((bsz, n_chunks, CHUNK) + t.shape[2:])

    x_c = to_chunks(x.astype(f32) * dt[..., None]).reshape(bsz, n_chunks, CHUNK, G, R, P)
    a_c = to_chunks(dt * A).reshape(bsz, n_chunks, CHUNK, G, R)
    B_c = to_chunks(Bm.astype(f32))
    C_c = to_chunks(Cm.astype(f32))
    a_cum = jnp.cumsum(a_c, axis=2)
    causal = jnp.tril(jnp.ones((CHUNK, CHUNK), bool))[:, :, None, None]
    seg = a_cum[:, :, :, None] - a_cum[:, :, None, :]
    Lmat = jnp.exp(jnp.where(causal, seg, -jnp.inf))
    CB = jnp.einsum('bclgn,bcsgn->bclsg', C_c, B_c)
    y_diag = jnp.einsum('bclsg,bclsgr,bcsgrp->bclgrp', CB, Lmat, x_c)
    decay_states = jnp.exp(a_cum[:, :, -1:] - a_cum)
    states = jnp.einsum('bcsgn,bcsgr,bcsgrp->bcgrpn', B_c, decay_states, x_c)
    chunk_decay = jnp.exp(a_cum[:, :, -1])

    def step(h, inp):
        st, dec = inp
        return h * dec[..., None, None] + st, h

    h_init = h0.astype(f32).reshape(bsz, G, R, P, N)
    h_final, prev = lax.scan(step, h_init, (jnp.moveaxis(states, 1, 0), jnp.moveaxis(chunk_decay, 1, 0)))
    prev = jnp.moveaxis(prev, 0, 1)
    y_off = jnp.einsum('bclgn,bcgrpn,bclgr->bclgrp', C_c, prev, jnp.exp(a_cum))
    y = (y_diag + y_off).reshape(bsz, n_chunks * CHUNK, SSM_HEADS, P)[:, :L]
    return y, h_final.reshape(bsz, SSM_HEADS, P, N)


def ssd_mixer(z, xBC, dt_raw, conv_past, h0, conv_w, conv_b, dt_bias, A_log, D_skip, norm_g):
    bsz, L = z.shape[:2]
    xBC, conv_new = causal_dwconv(xBC, conv_past, conv_w, conv_b)
    xBC = jax.nn.silu(xBC)
    xs, Bm, Cm = jnp.split(xBC, [D_SSM, D_SSM + SSM_GROUPS * SSM_STATE], axis=-1)
    xh = xs.reshape(bsz, L, SSM_HEADS, SSM_HEAD_DIM)
    Bm = Bm.reshape(bsz, L, SSM_GROUPS, SSM_STATE)
    Cm = Cm.reshape(bsz, L, SSM_GROUPS, SSM_STATE)
    dt = jax.nn.softplus(dt_raw.astype(jnp.float32) + dt_bias.astype(jnp.float32))
    A = -jnp.exp(A_log.astype(jnp.float32))
    y, h_new = ssd_scan(xh, dt, A, Bm, Cm, h0)
    y = y + xh.astype(jnp.float32) * D_skip.astype(jnp.float32)[:, None]
    y = y.reshape(bsz, L, D_SSM) * jax.nn.silu(z.astype(jnp.float32))
    return rms_norm(y, norm_g).astype(z.dtype), conv_new, h_new


def hybrid_layer(x, k_past, v_past, h0, conv_past, ffn_past,
                 w_in, rel_bias, attn_norm_g, ssm_conv_w, ssm_conv_b, ssm_dt_bias, ssm_A_log,
                 ssm_D, ssm_norm_g, w_out, ln1_g, ln1_b, w_up, ffn_conv_w, ffn_conv_b, w_down,
                 ln2_g, ln2_b):
    bsz, L, _ = x.shape
    if h0 is None:
        h0 = jnp.zeros((bsz, SSM_HEADS, SSM_HEAD_DIM, SSM_STATE), jnp.float32)
        conv_past = jnp.zeros((bsz, SSM_CONV - 1, SSM_CONV_DIM), x.dtype)
        ffn_past = jnp.zeros((bsz, FFN_CONV - 1, 2 * D_FF), x.dtype)
    proj = x @ w_in
    q, k, v, z, xBC, dt_raw = jnp.split(proj, SPLITS, axis=-1)
    q = q.reshape(bsz, L, ATTN_HEADS, ATTN_HEAD_DIM)
    k = k.reshape(bsz, L, ATTN_HEADS, ATTN_HEAD_DIM)
    v = v.reshape(bsz, L, ATTN_HEADS, ATTN_HEAD_DIM)
    if k_past is None:
        a = attn_prompt(q, k, v, rel_bias)
        n_keep = min(WINDOW, L)
        k_new, v_new = k[:, L - n_keep:], v[:, L - n_keep:]
    else:
        a = attn_sample(q, k, v, k_past, v_past, rel_bias)
        k_new, v_new = k, v
    a = rms_norm(a.reshape(bsz, L, D_ATTN), attn_norm_g)
    s, conv_new, h_new = ssd_mixer(z, xBC, dt_raw, conv_past, h0, ssm_conv_w, ssm_conv_b,
                                   ssm_dt_bias, ssm_A_log, ssm_D, ssm_norm_g)
    mix = jnp.concatenate([a, s], axis=-1) @ w_out
    x = layer_norm(ALPHA * x + mix, ln1_g, ln1_b)
    h, ffn_new = causal_dwconv(x @ w_up, ffn_past, ffn_conv_w, ffn_conv_b)
    hv, hg = jnp.split(h, [D_FF], axis=-1)
    f = (hv * jax.nn.silu(hg)) @ w_down
    x = layer_norm(ALPHA * x + f, ln2_g, ln2_b)
    return x, (k_new, v_new, h_new, conv_new, ffn_new)


def setup_inputs(seed: int = 0) -> dict:
    key = jax.random.key(seed)
    ks = jax.random.split(key, 26)
    f32 = jnp.float32

    def nrm(k, shape, scale):
        return jax.random.normal(k, shape, f32) * scale

    n_cache = min(WINDOW, PAST_LEN)
    dt0 = jnp.exp(jax.random.uniform(ks[9], (DEPTH, SSM_HEADS), f32, math.log(1e-3), math.log(1e-1)))
    return {
        'x_prompt': nrm(ks[0], (BATCH, SEQ, D_MODEL), 1.0),
        'x_sample': nrm(ks[1], (DEC_BATCH, DEC_SEQ, D_MODEL), 1.0),
        'cache_attn_k': nrm(ks[2], (DEPTH, DEC_BATCH, n_cache, ATTN_HEADS, ATTN_HEAD_DIM), 1.0),
        'cache_attn_v': nrm(ks[3], (DEPTH, DEC_BATCH, n_cache, ATTN_HEADS, ATTN_HEAD_DIM), 1.0),
        'state_ssm': nrm(ks[4], (DEPTH, DEC_BATCH, SSM_HEADS, SSM_HEAD_DIM, SSM_STATE), 0.5),
        'state_ssm_conv': nrm(ks[5], (DEPTH, DEC_BATCH, SSM_CONV - 1, SSM_CONV_DIM), 1.0),
        'state_ffn_conv': nrm(ks[6], (DEPTH, DEC_BATCH, FFN_CONV - 1, 2 * D_FF), 1.0),
        'w_in': nrm(ks[7], (DEPTH, D_MODEL, D_IN_PROJ), D_MODEL ** -0.5),
        'rel_bias': nrm(ks[8], (DEPTH, ATTN_HEADS, N_REL), 0.5),
        'attn_norm_g': 1.0 + nrm(ks[10], (DEPTH, D_ATTN), 0.02),
        'ssm_conv_w': nrm(ks[11], (DEPTH, SSM_CONV, SSM_CONV_DIM), SSM_CONV ** -0.5),
        'ssm_conv_b': nrm(ks[12], (DEPTH, SSM_CONV_DIM), 0.02),
        'ssm_dt_bias': dt0 + jnp.log(-jnp.expm1(-dt0)),
        'ssm_A_log': jnp.log(jax.random.uniform(ks[13], (DEPTH, SSM_HEADS), f32, 1.0, 16.0)),
        'ssm_D': 1.0 + nrm(ks[14], (DEPTH, SSM_HEADS), 0.02),
        'ssm_norm_g': 1.0 + nrm(ks[15], (DEPTH, D_SSM), 0.02),
        'w_out': nrm(ks[16], (DEPTH, D_MIX, D_MODEL), D_MIX ** -0.5 * BETA_INIT),
        'ln1_g': 1.0 + nrm(ks[17], (DEPTH, D_MODEL), 0.02),
        'ln1_b': nrm(ks[18], (DEPTH, D_MODEL), 0.02),
        'w_up': nrm(ks[19], (DEPTH, D_MODEL, 2 * D_FF), D_MODEL ** -0.5),
        'ffn_conv_w': nrm(ks[20], (DEPTH, FFN_CONV, 2 * D_FF), FFN_CONV ** -0.5),
        'ffn_conv_b': nrm(ks[21], (DEPTH, 2 * D_FF), 0.02),
        'w_down': nrm(ks[22], (DEPTH, D_FF, D_MODEL), D_FF ** -0.5 * BETA_INIT),
        'ln2_g': 1.0 + nrm(ks[23], (DEPTH, D_MODEL), 0.02),
        'ln2_b': nrm(ks[24], (DEPTH, D_MODEL), 0.02),
    }


def reference(x_prompt, x_sample, cache_attn_k, cache_attn_v, state_ssm, state_ssm_conv, state_ffn_conv,
              w_in, rel_bias, attn_norm_g, ssm_conv_w, ssm_conv_b, ssm_dt_bias, ssm_A_log, ssm_D,
              ssm_norm_g, w_out, ln1_g, ln1_b, w_up, ffn_conv_w, ffn_conv_b, w_down, ln2_g, ln2_b):
    yp, ys = x_prompt, x_sample
    st_p, st_s = [], []
    for l in range(DEPTH):
        params = (w_in[l], rel_bias[l], attn_norm_g[l], ssm_conv_w[l], ssm_conv_b[l], ssm_dt_bias[l],
                  ssm_A_log[l], ssm_D[l], ssm_norm_g[l], w_out[l], ln1_g[l], ln1_b[l], w_up[l],
                  ffn_conv_w[l], ffn_conv_b[l], w_down[l], ln2_g[l], ln2_b[l])
        yp, sp = hybrid_layer(yp, None, None, None, None, None, *params)
        ys, ss = hybrid_layer(ys, cache_attn_k[l], cache_attn_v[l], state_ssm[l], state_ssm_conv[l],
                              state_ffn_conv[l], *params)
        st_p.append(sp)
        st_s.append(ss)
    new_k_prompt = jnp.stack([s[0] for s in st_p])
    new_v_prompt = jnp.stack([s[1] for s in st_p])
    new_k_sample = jnp.stack([s[0] for s in st_s])
    new_v_sample = jnp.stack([s[1] for s in st_s])
    new_ssm_prompt = jnp.stack([s[2] for s in st_p])
    new_ssm_sample = jnp.stack([s[2] for s in st_s])
    new_ssm_conv_prompt = jnp.stack([s[3] for s in st_p])
    new_ssm_conv_sample = jnp.stack([s[3] for s in st_s])
    new_ffn_conv_prompt = jnp.stack([s[4] for s in st_p])
    new_ffn_conv_sample = jnp.stack([s[4] for s in st_s])
    return (yp, ys, new_k_prompt, new_v_prompt, new_k_sample, new_v_sample,
            new_ssm_prompt, new_ssm_sample, new_ssm_conv_prompt, new_ssm_conv_sample,
            new_ffn_conv_prompt, new_ffn_conv_sample)
```

```python
import functools

import jax
import jax.numpy as jnp
from jax import lax
from jax.experimental import pallas as pl
from jax.experimental.pallas import tpu as pltpu

F32 = jnp.float32
BF16 = jnp.bfloat16

D_MODEL = 1024
CHUNK = 64
WINDOW = 512
HEAD_DIM = 64
N_HEADS = 8
D_ATTN = N_HEADS * HEAD_DIM
ATTN_SCALE = HEAD_DIM ** -0.5
MAX_REL = 128
N_REL = 2 * MAX_REL + 1
NEG_INF = -1e30
SSM_HEADS = 8
SSM_GROUPS = 2
HEADS_PER_GROUP = SSM_HEADS // SSM_GROUPS
SSM_HEAD_DIM = 64
SSM_STATE = 128
D_SSM = SSM_HEADS * SSM_HEAD_DIM
SSM_CONV = 4
SSM_CONV_DIM = D_SSM + 2 * SSM_GROUPS * SSM_STATE
D_FF = 2688
FFN_CONV = 3
LN_EPS = 1e-5
RMS_EPS = 1e-5

LANES = 128
SUBLANES = 8
VMEM_LIMIT_BYTES = 56 * 1024 * 1024

COL_Q = 0
COL_K = D_ATTN
COL_V = 2 * D_ATTN
COL_Z = 3 * D_ATTN
COL_XBC = COL_Z + D_SSM
COL_DT = COL_XBC + SSM_CONV_DIM
W_IN_COLS = COL_DT + LANES
TAIL_ROWS = SUBLANES


def _nt_dot(a, b):
    return lax.dot_general(a, b, (((1,), (1,)), ((), ())), preferred_element_type=F32)


def _tn_dot(a, b):
    return lax.dot_general(a, b, (((0,), (0,)), ((), ())), preferred_element_type=F32)


def _dot(a, b):
    return jnp.dot(a, b, preferred_element_type=F32)


def _silu(x):
    return x * jax.nn.sigmoid(x)


def _softplus(x):
    return jnp.maximum(x, 0.0) + jnp.log1p(jnp.exp(-jnp.abs(x)))


def _layer_norm(u, g, b):
    mu = jnp.mean(u, axis=-1, keepdims=True)
    d = u - mu
    var = jnp.mean(d * d, axis=-1, keepdims=True)
    return d * lax.rsqrt(var + LN_EPS) * g + b


def _rms_norm(u, g):
    return u * lax.rsqrt(jnp.mean(u * u, axis=-1, keepdims=True) + RMS_EPS) * g


def _bias_kernel(rb_ref, out_ref):
    band = WINDOW + CHUNK
    i = lax.broadcasted_iota(jnp.int32, (CHUNK, band), 0)
    j = lax.broadcasted_iota(jnp.int32, (CHUNK, band), 1)
    idx = jnp.clip(WINDOW + i - j, -MAX_REL, MAX_REL) + MAX_REL
    for h in range(N_HEADS):
        def body(r, acc, h=h):
            return jnp.where(idx == r, rb_ref[h, r], acc)
        init = jnp.full((CHUNK, band), rb_ref[h, N_REL - 1], F32)
        out_ref[h] = lax.fori_loop(0, N_REL - 1, body, init)


def _bias_table(rel_bias):
    return pl.pallas_call(
        _bias_kernel,
        out_shape=jax.ShapeDtypeStruct((N_HEADS, CHUNK, WINDOW + CHUNK), F32),
        in_specs=[pl.BlockSpec(memory_space=pltpu.SMEM)],
        out_specs=pl.BlockSpec(memory_space=pltpu.VMEM),
        name="rel_bias_table",
    )(rel_bias)


def _mixer_kernel(*refs, tile, chunk, n_tiles, has_past, alpha):
    n_state_in = 4 if has_past else 0
    x_ref = refs[0]
    if has_past:
        kp_ref, vp_ref, h0_ref, cp_ref = refs[1:5]
    (w_in_ref, bias_ref, ag_ref, cw_ref, cb_ref, dtb_ref, alog_ref, dsk_ref, sg_ref, w_out_ref,
     l1g_ref, l1b_ref) = refs[1 + n_state_in:13 + n_state_in]
    x1_ref, ko_ref, vo_ref, ho_ref, co_ref = refs[13 + n_state_in:18 + n_state_in]
    (kwin, vwin, qe_s, qo_s, z_s, xp_s, xc_s, a_s, y_s, dt_s, ccol_s, crow_s, ht_s) = refs[18 + n_state_in:]

    n_chunks = tile // chunk
    band = WINDOW + chunk
    t = pl.program_id(1)

    @pl.when(t == 0)
    def _init():
        if has_past:
            kwin[0:WINDOW] = kp_ref[0].astype(BF16)
            vwin[0:WINDOW] = vp_ref[0].astype(BF16)
            for g in range(SSM_GROUPS):
                ht_s[g] = h0_ref[0, g].T
            xp_s[0:TAIL_ROWS] = jnp.zeros((TAIL_ROWS, SSM_CONV_DIM), F32)
            xp_s[TAIL_ROWS - (SSM_CONV - 1):TAIL_ROWS] = cp_ref[0]
        else:
            kwin[0:WINDOW] = jnp.zeros((WINDOW, D_ATTN), BF16)
            vwin[0:WINDOW] = jnp.zeros((WINDOW, D_ATTN), BF16)
            ht_s[...] = jnp.zeros(ht_s.shape, F32)
            xp_s[0:TAIL_ROWS] = jnp.zeros((TAIL_ROWS, SSM_CONV_DIM), F32)

    x = x_ref[0]
    xb = x.astype(BF16)

    def proj(lo, hi):
        return _dot(xb, w_in_ref[:, lo:hi])

    q = proj(COL_Q, COL_K) * ATTN_SCALE
    lane_q = lax.broadcasted_iota(jnp.int32, (tile, D_ATTN), 1)
    even_head = (lane_q & HEAD_DIM) == 0
    qe_s[...] = jnp.where(even_head, q, 0.0).astype(BF16)
    qo_s[...] = jnp.where(even_head, 0.0, q).astype(BF16)
    k = proj(COL_K, COL_V)
    ko_ref[0] = k
    kwin[WINDOW:WINDOW + tile] = k.astype(BF16)
    v = proj(COL_V, COL_Z)
    vo_ref[0] = v
    vwin[WINDOW:WINDOW + tile] = v.astype(BF16)
    z_s[...] = proj(COL_Z, COL_XBC)
    xp_s[TAIL_ROWS:TAIL_ROWS + tile] = proj(COL_XBC, COL_DT)
    dt_raw = proj(COL_DT, W_IN_COLS)

    base = TAIL_ROWS - (SSM_CONV - 1)
    for cblk in range(SSM_CONV_DIM // LANES):
        cs = slice(cblk * LANES, (cblk + 1) * LANES)
        acc = cb_ref[:, cs] + cw_ref[0:1, cs] * xp_s[base:base + tile, cs]
        for tap in range(1, SSM_CONV):
            acc = acc + cw_ref[tap:tap + 1, cs] * xp_s[base + tap:base + tap + tile, cs]
        xc_s[:, cs] = _silu(acc)
    new_tail = xp_s[TAIL_ROWS + tile - (SSM_CONV - 1):TAIL_ROWS + tile]
    co_ref[0] = new_tail
    xp_s[TAIL_ROWS - (SSM_CONV - 1):TAIL_ROWS] = new_tail

    dt = _softplus(dt_raw + dtb_ref[...])
    dt_s[...] = dt
    a = dt * (-jnp.exp(alog_ref[...]))
    pad_rows = ccol_s.shape[0]
    if pad_rows > tile:
        a = jnp.concatenate([a, jnp.zeros((pad_rows - tile, LANES), F32)], axis=0)
    li = lax.broadcasted_iota(jnp.int32, (LANES, LANES), 0)
    si = lax.broadcasted_iota(jnp.int32, (LANES, LANES), 1)
    shift = chunk.bit_length() - 1
    tril_bd = jnp.where((si <= li) & ((li >> shift) == (si >> shift)), 1.0, 0.0).astype(F32)
    for blk in range(pad_rows // LANES):
        rs = slice(blk * LANES, (blk + 1) * LANES)
        ccol_s[rs] = jnp.dot(tril_bd, a[rs], precision=lax.Precision.HIGHEST, preferred_element_type=F32)
    for c in range(n_chunks):
        blk = ccol_s[c * chunk:(c + 1) * chunk]
        if chunk < LANES:
            blk = jnp.concatenate([blk, jnp.zeros((LANES - chunk, LANES), F32)], axis=0)
        crow_s[c] = blk.T[0:SUBLANES]

    lane = lax.broadcasted_iota(jnp.int32, (chunk, LANES), 1)
    lo_half = lane < HEAD_DIM
    tri = (lax.broadcasted_iota(jnp.int32, (chunk, chunk), 0)
           >= lax.broadcasted_iota(jnp.int32, (chunk, chunk), 1))
    band_col = lax.broadcasted_iota(jnp.int32, (chunk, band), 1)
    state_lane = lax.broadcasted_iota(jnp.int32, (1, HEADS_PER_GROUP * SSM_HEAD_DIM), 1)

    def bcast(mat, col, width):
        return jnp.broadcast_to(mat[:, col:col + 1], (chunk, width))

    def pair(mat, h_even):
        return jnp.where(lo_half, bcast(mat, h_even, LANES), bcast(mat, h_even + 1, LANES))

    def chunk_body(c):
        r0 = c * chunk if isinstance(c, int) else pl.multiple_of(c * chunk, chunk)
        rows = pl.ds(r0, chunk)
        win = pl.ds(r0, band)

        if not has_past:
            valid = (t * tile + r0 - WINDOW + band_col) >= 0
        for j in range(N_HEADS // 2):
            cs = slice(j * LANES, (j + 1) * LANES)
            k2 = kwin[win, cs]
            v2 = vwin[win, cs]
            out2 = None
            for e, q_s in enumerate((qe_s, qo_s)):
                s = _nt_dot(q_s[rows, cs], k2) + bias_ref[2 * j + e, 0:chunk, 0:band]
                if not has_past:
                    s = jnp.where(valid, s, NEG_INF)
                m = jnp.max(s, axis=-1, keepdims=True)
                p = jnp.exp(s - m)
                denom = jnp.sum(p, axis=-1, keepdims=True)
                o = _dot(p.astype(BF16), v2) * (1.0 / denom)
                out2 = o if e == 0 else jnp.where(lo_half, out2, o)
            a_s[rows, cs] = out2

        dtc = dt_s[rows, :]
        ccol = ccol_s[rows, :]
        crow = crow_s[c]
        clast = ccol[chunk - 1:chunk, :]
        wcol = jnp.exp(clast - ccol) * dtc
        ecol = jnp.exp(ccol)
        cdec = jnp.exp(clast)
        for g in range(SSM_GROUPS):
            b_g = xc_s[rows, D_SSM + g * SSM_STATE:D_SSM + (g + 1) * SSM_STATE].astype(BF16)
            c_g = xc_s[rows, D_SSM + (SSM_GROUPS + g) * SSM_STATE:
                       D_SSM + (SSM_GROUPS + g + 1) * SSM_STATE].astype(BF16)
            cb = _nt_dot(c_g, b_g)
            h_t = ht_s[g]
            y_off = _dot(c_g, h_t.astype(BF16))
            xw_parts = []
            for jj in range(HEADS_PER_GROUP // 2):
                slab = g * (HEADS_PER_GROUP // 2) + jj
                h_even = HEADS_PER_GROUP * g + 2 * jj
                cs = slice(slab * LANES, (slab + 1) * LANES)
                x2 = xc_s[rows, cs]
                xdt2 = (x2 * pair(dtc, h_even)).astype(BF16)
                y_diag = None
                for e in range(2):
                    h = h_even + e
                    seg = bcast(ccol, h, chunk) - crow[h:h + 1, 0:chunk]
                    lmat = jnp.where(tri, jnp.exp(seg), 0.0)
                    yd = _dot((cb * lmat).astype(BF16), xdt2)
                    y_diag = yd if e == 0 else jnp.where(lo_half, y_diag, yd)
                y2 = (y_diag + y_off[:, jj * LANES:(jj + 1) * LANES] * pair(ecol, h_even)
                      + x2 * dsk_ref[:, cs])
                y_s[rows, cs] = y2
                xw_parts.append((x2 * pair(wcol, h_even)).astype(BF16))
            xw = jnp.concatenate(xw_parts, axis=1)
            dec = None
            for r in reversed(range(HEADS_PER_GROUP)):
                h = HEADS_PER_GROUP * g + r
                d_r = jnp.broadcast_to(cdec[:, h:h + 1], (1, HEADS_PER_GROUP * SSM_HEAD_DIM))
                dec = d_r if dec is None else jnp.where(state_lane < (r + 1) * SSM_HEAD_DIM, d_r, dec)
            ht_s[g] = h_t * dec + _tn_dot(b_g, xw)

    if n_chunks == 1:
        chunk_body(0)
    else:
        def loop_body(c, carry):
            chunk_body(c)
            return carry
        lax.fori_loop(0, n_chunks, loop_body, 0)

    for g in range(SSM_GROUPS):
        ho_ref[0, g] = ht_s[g].T

    if n_tiles > 1:
        kwin[0:WINDOW] = kwin[tile:tile + WINDOW]
        vwin[0:WINDOW] = vwin[tile:tile + WINDOW]

    a_n = _rms_norm(a_s[...], ag_ref[...]).astype(BF16)
    s_n = _rms_norm(y_s[...] * _silu(z_s[...]), sg_ref[...]).astype(BF16)
    mix = _dot(a_n, w_out_ref[0:D_ATTN, :]) + _dot(s_n, w_out_ref[D_ATTN:D_ATTN + D_SSM, :])
    x1_ref[0] = _layer_norm(alpha * x + mix, l1g_ref[...], l1b_ref[...])


def _const_spec(shape):
    zeros = (0,) * len(shape)
    return pl.BlockSpec(shape, lambda b, t: zeros, pipeline_mode=pl.Buffered(1))


def _mixer(x, past, weights, *, tile, chunk, alpha):
    bsz, seq, _ = x.shape
    n_tiles = seq // tile
    assert seq % tile == 0 and tile % chunk == 0
    has_past = past is not None
    keep = min(WINDOW, seq)
    assert keep == tile, "the kept K/V rows must be exactly the last tile"
    pad_rows = max(tile, LANES)
    n_chunks = tile // chunk

    def per_stream(shape):
        zeros = (0,) * (len(shape) - 1)
        return pl.BlockSpec((1,) + tuple(shape[1:]), lambda b, t: (b,) + zeros)

    in_specs = [pl.BlockSpec((1, tile, D_MODEL), lambda b, t: (b, t, 0))]
    args = [x]
    if has_past:
        for arr in past:
            in_specs.append(per_stream(arr.shape))
            args.append(arr)
    for w in weights:
        in_specs.append(_const_spec(w.shape))
        args.append(w)

    out_shape = (
        jax.ShapeDtypeStruct((bsz, seq, D_MODEL), F32),
        jax.ShapeDtypeStruct((bsz, keep, D_ATTN), F32),
        jax.ShapeDtypeStruct((bsz, keep, D_ATTN), F32),
        jax.ShapeDtypeStruct((bsz, SSM_GROUPS, HEADS_PER_GROUP * SSM_HEAD_DIM, SSM_STATE), F32),
        jax.ShapeDtypeStruct((bsz, SSM_CONV - 1, SSM_CONV_DIM), F32),
    )
    out_specs = (
        pl.BlockSpec((1, tile, D_MODEL), lambda b, t: (b, t, 0)),
        per_stream(out_shape[1].shape),
        per_stream(out_shape[2].shape),
        per_stream(out_shape[3].shape),
        per_stream(out_shape[4].shape),
    )
    scratch = [
        pltpu.VMEM((WINDOW + tile, D_ATTN), BF16),
        pltpu.VMEM((WINDOW + tile, D_ATTN), BF16),
        pltpu.VMEM((tile, D_ATTN), BF16),
        pltpu.VMEM((tile, D_ATTN), BF16),
        pltpu.VMEM((tile, D_SSM), F32),
        pltpu.VMEM((TAIL_ROWS + tile, SSM_CONV_DIM), F32),
        pltpu.VMEM((tile, SSM_CONV_DIM), F32),
        pltpu.VMEM((tile, D_ATTN), F32),
        pltpu.VMEM((tile, D_SSM), F32),
        pltpu.VMEM((tile, LANES), F32),
        pltpu.VMEM((pad_rows, LANES), F32),
        pltpu.VMEM((n_chunks, SUBLANES, LANES), F32),
        pltpu.VMEM((SSM_GROUPS, SSM_STATE, HEADS_PER_GROUP * SSM_HEAD_DIM), F32),
    ]
    return pl.pallas_call(
        functools.partial(_mixer_kernel, tile=tile, chunk=chunk, n_tiles=n_tiles, has_past=has_past,
                          alpha=alpha),
        grid=(bsz, n_tiles),
        in_specs=in_specs,
        out_specs=out_specs,
        out_shape=out_shape,
        scratch_shapes=scratch,
        compiler_params=pltpu.CompilerParams(
            dimension_semantics=("arbitrary", "arbitrary"),
            vmem_limit_bytes=VMEM_LIMIT_BYTES),
        name="mixer_past" if has_past else "mixer",
    )(*args)


def _ffn_kernel(*refs, tile, has_past, alpha):
    x1_ref = refs[0]
    n_in = 1
    if has_past:
        fp_ref = refs[1]
        n_in = 2
    w_up_ref, fcw_ref, fcb_ref, w_dn_ref, l2g_ref, l2b_ref = refs[n_in:n_in + 6]
    y_ref, fo_ref = refs[n_in + 6:n_in + 8]
    h_s, g_s = refs[n_in + 8:]
    t = pl.program_id(1)
    hist = FFN_CONV - 1

    @pl.when(t == 0)
    def _init():
        h_s[0:TAIL_ROWS] = jnp.zeros((TAIL_ROWS, 2 * D_FF), F32)
        if has_past:
            h_s[TAIL_ROWS - hist:TAIL_ROWS] = fp_ref[0]

    x1 = x1_ref[0]
    h_s[TAIL_ROWS:TAIL_ROWS + tile] = _dot(x1.astype(BF16), w_up_ref[...])

    base = TAIL_ROWS - hist

    def conv(cs):
        acc = fcb_ref[:, cs] + fcw_ref[0:1, cs] * h_s[base:base + tile, cs]
        for tap in range(1, FFN_CONV):
            acc = acc + fcw_ref[tap:tap + 1, cs] * h_s[base + tap:base + tap + tile, cs]
        return acc

    for cblk in range(D_FF // LANES):
        cs_v = slice(cblk * LANES, (cblk + 1) * LANES)
        cs_g = slice(D_FF + cblk * LANES, D_FF + (cblk + 1) * LANES)
        g_s[:, cs_v] = (conv(cs_v) * _silu(conv(cs_g))).astype(BF16)

    new_tail = h_s[TAIL_ROWS + tile - hist:TAIL_ROWS + tile]
    fo_ref[0] = new_tail
    h_s[TAIL_ROWS - hist:TAIL_ROWS] = new_tail

    f = _dot(g_s[...], w_dn_ref[...])
    y_ref[0] = _layer_norm(alpha * x1 + f, l2g_ref[...], l2b_ref[...])


def _conv_ffn(x1, ffn_past, weights, *, tile, alpha):
    bsz, seq, _ = x1.shape
    assert seq % tile == 0
    has_past = ffn_past is not None
    in_specs = [pl.BlockSpec((1, tile, D_MODEL), lambda b, t: (b, t, 0))]
    args = [x1]
    if has_past:
        in_specs.append(pl.BlockSpec((1, FFN_CONV - 1, 2 * D_FF), lambda b, t: (b, 0, 0)))
        args.append(ffn_past)
    for w in weights:
        in_specs.append(_const_spec(w.shape))
        args.append(w)
    out_shape = (
        jax.ShapeDtypeStruct((bsz, seq, D_MODEL), F32),
        jax.ShapeDtypeStruct((bsz, FFN_CONV - 1, 2 * D_FF), F32),
    )
    out_specs = (
        pl.BlockSpec((1, tile, D_MODEL), lambda b, t: (b, t, 0)),
        pl.BlockSpec((1, FFN_CONV - 1, 2 * D_FF), lambda b, t: (b, 0, 0)),
    )
    scratch = [
        pltpu.VMEM((TAIL_ROWS + tile, 2 * D_FF), F32),
        pltpu.VMEM((tile, D_FF), BF16),
    ]
    return pl.pallas_call(
        functools.partial(_ffn_kernel, tile=tile, has_past=has_past, alpha=alpha),
        grid=(bsz, seq // tile),
        in_specs=in_specs,
        out_specs=out_specs,
        out_shape=out_shape,
        scratch_shapes=scratch,
        compiler_params=pltpu.CompilerParams(
            dimension_semantics=("arbitrary", "arbitrary"),
            vmem_limit_bytes=VMEM_LIMIT_BYTES),
        name="conv_ffn_past" if has_past else "conv_ffn",
    )(*args)


def _pad_lanes(vec):
    return jnp.pad(vec.astype(F32), (0, LANES - vec.shape[0]))[None, :]


def _row(vec):
    return vec.astype(F32)[None, :]


def kernel(x_prompt, x_sample, cache_attn_k, cache_attn_v, state_ssm, state_ssm_conv, state_ffn_conv,
           w_in, rel_bias, attn_norm_g, ssm_conv_w, ssm_conv_b, ssm_dt_bias, ssm_A_log, ssm_D,
           ssm_norm_g, w_out, ln1_g, ln1_b, w_up, ffn_conv_w, ffn_conv_b, w_down, ln2_g, ln2_b):
    depth = w_in.shape[0]
    alpha = (2.0 * depth) ** 0.25
    n_dec, dec_seq, _ = x_sample.shape
    n_cache = cache_attn_k.shape[2]
    assert n_cache == WINDOW

    yp, ys = x_prompt, x_sample
    outs_p, outs_s = [], []
    for l in range(depth):
        w_in_p = jnp.pad(w_in[l], ((0, 0), (0, W_IN_COLS - w_in.shape[2]))).astype(BF16)
        mixer_w = (
            w_in_p,
            _bias_table(rel_bias[l]),
            _row(attn_norm_g[l]),
            ssm_conv_w[l].astype(F32),
            _row(ssm_conv_b[l]),
            _pad_lanes(ssm_dt_bias[l]),
            _pad_lanes(ssm_A_log[l]),
            _row(jnp.repeat(ssm_D[l], SSM_HEAD_DIM)),
            _row(ssm_norm_g[l]),
            w_out[l].astype(BF16),
            _row(ln1_g[l]),
            _row(ln1_b[l]),
        )
        ffn_w = (
            w_up[l].astype(BF16),
            ffn_conv_w[l].astype(F32),
            _row(ffn_conv_b[l]),
            w_down[l].astype(BF16),
            _row(ln2_g[l]),
            _row(ln2_b[l]),
        )
        x1, kp, vp, hp, cp = _mixer(yp, None, mixer_w, tile=WINDOW, chunk=CHUNK, alpha=alpha)
        yp, fp = _conv_ffn(x1, None, ffn_w, tile=256, alpha=alpha)
        outs_p.append((kp, vp, hp, cp, fp))
        past = (
            cache_attn_k[l].reshape(n_dec, n_cache, D_ATTN),
            cache_attn_v[l].reshape(n_dec, n_cache, D_ATTN),
            state_ssm[l].reshape(n_dec, SSM_GROUPS, HEADS_PER_GROUP * SSM_HEAD_DIM, SSM_STATE),
            state_ssm_conv[l],
        )
        x1, ks, vs, hs, cs = _mixer(ys, past, mixer_w, tile=dec_seq, chunk=dec_seq, alpha=alpha)
        ys, fs = _conv_ffn(x1, state_ffn_conv[l], ffn_w, tile=dec_seq, alpha=alpha)
        outs_s.append((ks, vs, hs, cs, fs))

    def stack(outs, idx, shape_tail):
        return jnp.stack([o[idx].reshape((o[idx].shape[0],) + shape_tail) for o in outs])

    kv_p = (min(WINDOW, x_prompt.shape[1]), N_HEADS, HEAD_DIM)
    kv_s = (dec_seq, N_HEADS, HEAD_DIM)
    st = (SSM_HEADS, SSM_HEAD_DIM, SSM_STATE)
    return (
        yp, ys,
        stack(outs_p, 0, kv_p), stack(outs_p, 1, kv_p),
        stack(outs_s, 0, kv_s), stack(outs_s, 1, kv_s),
        stack(outs_p, 2, st), stack(outs_s, 2, st),
        stack(outs_p, 3, (SSM_CONV - 1, SSM_CONV_DIM)), stack(outs_s, 3, (SSM_CONV - 1, SSM_CONV_DIM)),
        stack(outs_p, 4, (FFN_CONV - 1, 2 * D_FF)), stack(outs_s, 4, (FFN_CONV - 1, 2 * D_FF)),
    )
```

```python
import functools

import jax
import jax.numpy as jnp
from jax import lax
from jax.experimental import pallas as pl
from jax.experimental.pallas import tpu as pltpu

F32 = jnp.float32
BF16 = jnp.bfloat16

D_MODEL = 1024
CHUNK = 64
WINDOW = 512
HEAD_DIM = 64
N_HEADS = 8
D_ATTN = N_HEADS * HEAD_DIM
ATTN_SCALE = HEAD_DIM ** -0.5
HEADS_PER_ATTN_GROUP = 4
ATTN_GROUPS = N_HEADS // HEADS_PER_ATTN_GROUP
GROUP_LANES = HEADS_PER_ATTN_GROUP * HEAD_DIM
SOFTMAX_ROWS = 32
KEY_TILE = 256
MAX_REL = 128
N_REL = 2 * MAX_REL + 1
NEG_INF = -1e30
SSM_HEADS = 8
SSM_GROUPS = 2
HEADS_PER_GROUP = SSM_HEADS // SSM_GROUPS
SSM_HEAD_DIM = 64
SSM_STATE = 128
D_SSM = SSM_HEADS * SSM_HEAD_DIM
SSM_CONV = 4
SSM_CONV_DIM = D_SSM + 2 * SSM_GROUPS * SSM_STATE
D_FF = 2688
FFN_CONV = 3
LN_EPS = 1e-5
RMS_EPS = 1e-5

LANES = 128
SUBLANES = 8
VMEM_LIMIT_BYTES = 56 * 1024 * 1024

COL_Q = 0
COL_K = D_ATTN
COL_V = 2 * D_ATTN
COL_Z = 3 * D_ATTN
COL_XBC = COL_Z + D_SSM
COL_DT = COL_XBC + SSM_CONV_DIM
W_IN_COLS = COL_DT + LANES
TAIL_ROWS = SUBLANES


def _round_up(n, m):
    return -(-n // m) * m


def _nt_dot(a, b):
    return lax.dot_general(a, b, (((1,), (1,)), ((), ())), preferred_element_type=F32)


def _tn_dot(a, b):
    return lax.dot_general(a, b, (((0,), (0,)), ((), ())), preferred_element_type=F32)


def _dot(a, b):
    return jnp.dot(a, b, preferred_element_type=F32)


def _silu(x):
    return x * jax.nn.sigmoid(x)


def _softplus(x):
    return jnp.maximum(x, 0.0) + jnp.log1p(jnp.exp(-jnp.abs(x)))


def _layer_norm(u, g, b):
    mu = jnp.mean(u, axis=-1, keepdims=True)
    d = u - mu
    var = jnp.mean(d * d, axis=-1, keepdims=True)
    return d * lax.rsqrt(var + LN_EPS) * g + b


def _rms_norm(u, g):
    return u * lax.rsqrt(jnp.mean(u * u, axis=-1, keepdims=True) + RMS_EPS) * g


def _bias_kernel(rb_ref, out_ref):
    band = WINDOW + CHUNK
    i = lax.broadcasted_iota(jnp.int32, (CHUNK, band), 0)
    j = lax.broadcasted_iota(jnp.int32, (CHUNK, band), 1)
    idx = jnp.clip(WINDOW + i - j, -MAX_REL, MAX_REL) + MAX_REL
    for h in range(N_HEADS):
        def body(r, acc, h=h):
            return jnp.where(idx == r, rb_ref[h, r], acc)
        init = jnp.full((CHUNK, band), rb_ref[h, N_REL - 1], F32)
        out_ref[h] = lax.fori_loop(0, N_REL - 1, body, init)


def _bias_table(rel_bias):
    return pl.pallas_call(
        _bias_kernel,
        out_shape=jax.ShapeDtypeStruct((N_HEADS, CHUNK, WINDOW + CHUNK), F32),
        in_specs=[pl.BlockSpec(memory_space=pltpu.SMEM)],
        out_specs=pl.BlockSpec(memory_space=pltpu.VMEM),
        name="rel_bias_table",
    )(rel_bias)


def _mixer_kernel(*refs, tile, chunk, n_tiles, has_past, alpha):
    n_state_in = 4 if has_past else 0
    x_ref = refs[0]
    if has_past:
        kp_ref, vp_ref, h0_ref, cp_ref = refs[1:5]
    (w_in_ref, bias_ref, ag_ref, cw_ref, cb_ref, dtb_ref, alog_ref, dsk_ref, sg_ref, w_out_ref,
     l1g_ref, l1b_ref) = refs[1 + n_state_in:13 + n_state_in]
    x1_ref, ko_ref, vo_ref, ho_ref, co_ref = refs[13 + n_state_in:18 + n_state_in]
    (kwin, vwin, q_s, qrep_s, z_s, xp_s, xc_s, a_s, y_s, dt_s, ccol_s, crow_s, drow_s, ht_s) = refs[
        18 + n_state_in:32 + n_state_in]
    stage = refs[32 + n_state_in:]
    s_bufs = (stage[0:2], stage[2:4])
    m_bufs = (stage[4:6], stage[6:8])
    p_bufs = (stage[8:10], stage[10:12])
    lp_bufs = (stage[12:14], stage[14:16])

    n_chunks = tile // chunk
    band = WINDOW + chunk
    band_pad = _round_up(band, LANES)
    t = pl.program_id(1)

    @pl.when(t == 0)
    def _init():
        kwin[WINDOW + tile:] = jnp.zeros((band_pad - band, D_ATTN), BF16)
        vwin[WINDOW + tile:] = jnp.zeros((band_pad - band, D_ATTN), BF16)
        if has_past:
            kwin[0:WINDOW] = kp_ref[0].astype(BF16)
            vwin[0:WINDOW] = vp_ref[0].astype(BF16)
            for g in range(SSM_GROUPS):
                ht_s[g] = h0_ref[0, g].T
            xp_s[0:TAIL_ROWS] = jnp.zeros((TAIL_ROWS, SSM_CONV_DIM), F32)
            xp_s[TAIL_ROWS - (SSM_CONV - 1):TAIL_ROWS] = cp_ref[0]
        else:
            kwin[0:WINDOW] = jnp.zeros((WINDOW, D_ATTN), BF16)
            vwin[0:WINDOW] = jnp.zeros((WINDOW, D_ATTN), BF16)
            ht_s[...] = jnp.zeros(ht_s.shape, F32)
            xp_s[0:TAIL_ROWS] = jnp.zeros((TAIL_ROWS, SSM_CONV_DIM), F32)

    if n_tiles > 1:
        @pl.when(t > 0)
        def _slide_window():
            kwin[0:WINDOW] = kwin[tile:tile + WINDOW]
            vwin[0:WINDOW] = vwin[tile:tile + WINDOW]

    x = x_ref[0]
    xb = x.astype(BF16)

    def proj(lo, hi):
        return _dot(xb, w_in_ref[:, lo:hi])

    q_s[...] = proj(COL_Q, COL_K) * ATTN_SCALE
    head_of_lane = lax.broadcasted_iota(jnp.int32, (chunk, GROUP_LANES), 1) // HEAD_DIM
    for c in range(n_chunks):
        for g in range(ATTN_GROUPS):
            q_cg = q_s[c * chunk:(c + 1) * chunk, g * GROUP_LANES:(g + 1) * GROUP_LANES]
            qrep_s[c, g] = jnp.concatenate(
                [jnp.where(head_of_lane == h, q_cg, 0.0) for h in range(HEADS_PER_ATTN_GROUP)],
                axis=0).astype(BF16)
    k = proj(COL_K, COL_V)
    ko_ref[0] = k
    kwin[WINDOW:WINDOW + tile] = k.astype(BF16)
    v = proj(COL_V, COL_Z)
    vo_ref[0] = v
    vwin[WINDOW:WINDOW + tile] = v.astype(BF16)
    z_s[...] = proj(COL_Z, COL_XBC)
    xp_s[TAIL_ROWS:TAIL_ROWS + tile] = proj(COL_XBC, COL_DT)
    dt_raw = proj(COL_DT, W_IN_COLS)

    base = TAIL_ROWS - (SSM_CONV - 1)
    for cblk in range(SSM_CONV_DIM // LANES):
        cs = slice(cblk * LANES, (cblk + 1) * LANES)
        acc = cb_ref[:, cs] + cw_ref[0:1, cs] * xp_s[base:base + tile, cs]
        for tap in range(1, SSM_CONV):
            acc = acc + cw_ref[tap:tap + 1, cs] * xp_s[base + tap:base + tap + tile, cs]
        xc_s[:, cs] = _silu(acc)
    new_tail = xp_s[TAIL_ROWS + tile - (SSM_CONV - 1):TAIL_ROWS + tile]
    co_ref[0] = new_tail
    xp_s[TAIL_ROWS - (SSM_CONV - 1):TAIL_ROWS] = new_tail

    dt = _softplus(dt_raw + dtb_ref[...])
    dt_s[...] = dt
    a = dt * (-jnp.exp(alog_ref[...]))
    pad_rows = ccol_s.shape[0]
    if pad_rows > tile:
        a = jnp.concatenate([a, jnp.zeros((pad_rows - tile, LANES), F32)], axis=0)
    li = lax.broadcasted_iota(jnp.int32, (LANES, LANES), 0)
    si = lax.broadcasted_iota(jnp.int32, (LANES, LANES), 1)
    shift = chunk.bit_length() - 1
    tril_bd = jnp.where((si <= li) & ((li >> shift) == (si >> shift)), 1.0, 0.0).astype(F32)
    for blk in range(pad_rows // LANES):
        rs = slice(blk * LANES, (blk + 1) * LANES)
        ccol_s[rs] = jnp.dot(tril_bd, a[rs], precision=lax.Precision.HIGHEST, preferred_element_type=F32)
    def to_row_form(blk):
        if chunk < LANES:
            blk = jnp.concatenate([blk, jnp.zeros((LANES - chunk, LANES), F32)], axis=0)
        return blk.T[0:SUBLANES]

    for c in range(n_chunks):
        crow_s[c] = to_row_form(ccol_s[c * chunk:(c + 1) * chunk])
        drow_s[c] = to_row_form(dt_s[c * chunk:(c + 1) * chunk])

    lane = lax.broadcasted_iota(jnp.int32, (chunk, LANES), 1)
    lo_half = lane < HEAD_DIM
    tri = (lax.broadcasted_iota(jnp.int32, (chunk, chunk), 0)
           >= lax.broadcasted_iota(jnp.int32, (chunk, chunk), 1))
    band_col = lax.broadcasted_iota(jnp.int32, (1, band_pad), 1)
    state_lane = lax.broadcasted_iota(jnp.int32, (1, HEADS_PER_GROUP * SSM_HEAD_DIM), 1)

    def bcast(mat, col, width):
        return jnp.broadcast_to(mat[:, col:col + 1], (chunk, width))

    def pair(mat, h_even):
        return jnp.where(lo_half, bcast(mat, h_even, LANES), bcast(mat, h_even + 1, LANES))

    def row_start(c):
        return c * chunk if isinstance(c, int) else pl.multiple_of(c * chunk, chunk)

    rep_rows = HEADS_PER_ATTN_GROUP * chunk
    group_lane = lax.broadcasted_iota(jnp.int32, (chunk, GROUP_LANES), 1)

    def attn_scores(c, g, s_ref, m_ref):
        r0 = row_start(c)
        q_rep = qrep_s[c, g]
        m_run = None
        for lo in range(0, band_pad, KEY_TILE):
            hi = min(lo + KEY_TILE, band_pad)
            k_t = kwin[pl.ds(pl.multiple_of(r0 + lo, chunk), hi - lo),
                       g * GROUP_LANES:(g + 1) * GROUP_LANES]
            s = _nt_dot(q_rep, k_t) + bias_ref[g, :, lo:hi]
            col = band_col[:, lo:hi]
            if has_past:
                valid = col < band
            else:
                valid = (col < band) & ((t * tile + r0 - WINDOW + col) >= 0)
            s = jnp.where(valid, s, NEG_INF)
            s_ref[:, lo:hi] = s
            for j in range((hi - lo) // LANES):
                s_j = s[:, j * LANES:(j + 1) * LANES]
                m_run = s_j if m_run is None else jnp.maximum(m_run, s_j)
            yield
        m_ref[...] = jnp.broadcast_to(jnp.max(m_run, axis=-1, keepdims=True), (rep_rows, LANES))
        yield

    def attn_softmax(s_ref, m_ref, p_ref, lp_ref):
        for blk in range(rep_rows // SOFTMAX_ROWS):
            rs = slice(blk * SOFTMAX_ROWS, (blk + 1) * SOFTMAX_ROWS)
            m = m_ref[rs, :]
            p_tiles = [jnp.exp(s_ref[rs, j * LANES:(j + 1) * LANES] - m) for j in range(band_pad // LANES)]
            lp = p_tiles[0]
            for p_j in p_tiles[1:]:
                lp = lp + p_j
            lp_ref[rs, :] = lp
            p_ref[rs, :] = jnp.concatenate(p_tiles, axis=1).astype(BF16)
            yield

    def attn_output(c, g, p_ref, lp_ref):
        r0 = row_start(c)
        inv = 1.0 / jnp.sum(lp_ref[...], axis=-1, keepdims=True)
        yield
        v_g = vwin[pl.ds(r0, band_pad), g * GROUP_LANES:(g + 1) * GROUP_LANES]
        o_all = _dot(p_ref[...], v_g) * inv
        o = o_all[(HEADS_PER_ATTN_GROUP - 1) * chunk:]
        for h in reversed(range(HEADS_PER_ATTN_GROUP - 1)):
            o = jnp.where(group_lane < (h + 1) * HEAD_DIM, o_all[h * chunk:(h + 1) * chunk], o)
        a_s[pl.ds(r0, chunk), g * GROUP_LANES:(g + 1) * GROUP_LANES] = o
        yield

    def ssd_chunk(c):
        r0 = row_start(c)
        rows = pl.ds(r0, chunk)
        dtc = dt_s[rows, :]
        ccol = ccol_s[rows, :]
        crow = crow_s[c]
        drow = drow_s[c]
        clast = ccol[chunk - 1:chunk, :]
        wcol = jnp.exp(clast - ccol) * dtc
        ecol = jnp.exp(ccol)
        cdec = jnp.exp(clast)
        yield
        for g in range(SSM_GROUPS):
            b_g = xc_s[rows, D_SSM + g * SSM_STATE:D_SSM + (g + 1) * SSM_STATE].astype(BF16)
            c_g = xc_s[rows, D_SSM + (SSM_GROUPS + g) * SSM_STATE:
                       D_SSM + (SSM_GROUPS + g + 1) * SSM_STATE].astype(BF16)
            cb = _nt_dot(c_g, b_g)
            h_t = ht_s[g]
            y_off = _dot(c_g, h_t.astype(BF16))
            xw_parts = []
            for jj in range(HEADS_PER_GROUP // 2):
                slab = g * (HEADS_PER_GROUP // 2) + jj
                h_even = HEADS_PER_GROUP * g + 2 * jj
                cs = slice(slab * LANES, (slab + 1) * LANES)
                x2 = xc_s[rows, cs]
                x2_bf = x2.astype(BF16)
                y_diag = None
                for e in range(2):
                    h = h_even + e
                    seg = bcast(ccol, h, chunk) - crow[h:h + 1, 0:chunk]
                    lmat = jnp.where(tri, jnp.exp(seg), 0.0) * drow[h:h + 1, 0:chunk]
                    yd = _dot((cb * lmat).astype(BF16), x2_bf)
                    y_diag = yd if e == 0 else jnp.where(lo_half, y_diag, yd)
                y2 = (y_diag + y_off[:, jj * LANES:(jj + 1) * LANES] * pair(ecol, h_even)
                      + x2 * dsk_ref[:, cs])
                y_s[rows, cs] = y2
                xw_parts.append((x2 * pair(wcol, h_even)).astype(BF16))
                yield
            xw = jnp.concatenate(xw_parts, axis=1)
            dec = None
            for r in reversed(range(HEADS_PER_GROUP)):
                h = HEADS_PER_GROUP * g + r
                d_r = jnp.broadcast_to(cdec[:, h:h + 1], (1, HEADS_PER_GROUP * SSM_HEAD_DIM))
                dec = d_r if dec is None else jnp.where(state_lane < (r + 1) * SSM_HEAD_DIM, d_r, dec)
            ht_s[g] = h_t * dec + _tn_dot(b_g, xw)
            yield

    def interleave(stages):
        stages = list(stages)
        while stages:
            for stage in list(stages):
                try:
                    next(stage)
                except StopIteration:
                    stages.remove(stage)

    def step(c, slot, scores_next, softmax_cur, output_prev):
        stages = []
        for g in range(ATTN_GROUPS):
            if scores_next:
                stages.append(attn_scores(c + 1, g, s_bufs[1 - slot][g], m_bufs[1 - slot][g]))
            if softmax_cur:
                stages.append(attn_softmax(s_bufs[slot][g], m_bufs[slot][g], p_bufs[slot][g],
                                           lp_bufs[slot][g]))
            if output_prev:
                stages.append(attn_output(c - 1, g, p_bufs[1 - slot][g], lp_bufs[1 - slot][g]))
        stages.append(ssd_chunk(c))
        interleave(stages)

    interleave(attn_scores(0, g, s_bufs[0][g], m_bufs[0][g]) for g in range(ATTN_GROUPS))
    if n_chunks == 1:
        step(0, 0, False, True, False)
        interleave(attn_output(0, g, p_bufs[0][g], lp_bufs[0][g]) for g in range(ATTN_GROUPS))
    else:
        assert n_chunks % 2 == 0
        step(0, 0, True, True, False)

        def pair_body(kk, carry):
            c = 2 * kk + 1
            step(c, 1, True, True, True)
            step(c + 1, 0, True, True, True)
            return carry
        lax.fori_loop(0, (n_chunks - 2) // 2, pair_body, 0)
        step(n_chunks - 1, 1, False, True, True)
        interleave(attn_output(n_chunks - 1, g, p_bufs[1][g], lp_bufs[1][g]) for g in range(ATTN_GROUPS))

    for g in range(SSM_GROUPS):
        ho_ref[0, g] = ht_s[g].T

    a_n = _rms_norm(a_s[...], ag_ref[...]).astype(BF16)
    s_n = _rms_norm(y_s[...] * _silu(z_s[...]), sg_ref[...]).astype(BF16)
    mix = _dot(a_n, w_out_ref[0:D_ATTN, :]) + _dot(s_n, w_out_ref[D_ATTN:D_ATTN + D_SSM, :])
    x1_ref[0] = _layer_norm(alpha * x + mix, l1g_ref[...], l1b_ref[...])


def _const_spec(shape):
    zeros = (0,) * len(shape)
    return pl.BlockSpec(shape, lambda b, t: zeros, pipeline_mode=pl.Buffered(1))


def _mixer(x, past, weights, *, tile, chunk, alpha):
    bsz, seq, _ = x.shape
    n_tiles = seq // tile
    assert seq % tile == 0 and tile % chunk == 0
    has_past = past is not None
    keep = min(WINDOW, seq)
    assert keep == tile, "the kept K/V rows must be exactly the last tile"
    pad_rows = max(tile, LANES)
    n_chunks = tile // chunk
    rep_rows = HEADS_PER_ATTN_GROUP * chunk
    band = WINDOW + chunk
    band_pad = _round_up(band, LANES)

    def per_stream(shape):
        zeros = (0,) * (len(shape) - 1)
        return pl.BlockSpec((1,) + tuple(shape[1:]), lambda b, t: (b,) + zeros)

    in_specs = [pl.BlockSpec((1, tile, D_MODEL), lambda b, t: (b, t, 0))]
    args = [x]
    if has_past:
        for arr in past:
            in_specs.append(per_stream(arr.shape))
            args.append(arr)
    for w in weights:
        in_specs.append(_const_spec(w.shape))
        args.append(w)

    out_shape = (
        jax.ShapeDtypeStruct((bsz, seq, D_MODEL), F32),
        jax.ShapeDtypeStruct((bsz, keep, D_ATTN), F32),
        jax.ShapeDtypeStruct((bsz, keep, D_ATTN), F32),
        jax.ShapeDtypeStruct((bsz, SSM_GROUPS, HEADS_PER_GROUP * SSM_HEAD_DIM, SSM_STATE), F32),
        jax.ShapeDtypeStruct((bsz, SSM_CONV - 1, SSM_CONV_DIM), F32),
    )
    out_specs = (
        pl.BlockSpec((1, tile, D_MODEL), lambda b, t: (b, t, 0)),
        per_stream(out_shape[1].shape),
        per_stream(out_shape[2].shape),
        per_stream(out_shape[3].shape),
        per_stream(out_shape[4].shape),
    )
    win_rows = WINDOW + tile + band_pad - band
    scratch = [
        pltpu.VMEM((win_rows, D_ATTN), BF16),
        pltpu.VMEM((win_rows, D_ATTN), BF16),
        pltpu.VMEM((tile, D_ATTN), F32),
        pltpu.VMEM((n_chunks, ATTN_GROUPS, rep_rows, GROUP_LANES), BF16),
        pltpu.VMEM((tile, D_SSM), F32),
        pltpu.VMEM((TAIL_ROWS + tile, SSM_CONV_DIM), F32),
        pltpu.VMEM((tile, SSM_CONV_DIM), F32),
        pltpu.VMEM((tile, D_ATTN), F32),
        pltpu.VMEM((tile, D_SSM), F32),
        pltpu.VMEM((tile, LANES), F32),
        pltpu.VMEM((pad_rows, LANES), F32),
        pltpu.VMEM((n_chunks, SUBLANES, LANES), F32),
        pltpu.VMEM((n_chunks, SUBLANES, LANES), F32),
        pltpu.VMEM((SSM_GROUPS, SSM_STATE, HEADS_PER_GROUP * SSM_HEAD_DIM), F32),
    ]
    n_stage = 2 * ATTN_GROUPS
    scratch += [pltpu.VMEM((rep_rows, band_pad), F32) for _ in range(n_stage)]
    scratch += [pltpu.VMEM((rep_rows, LANES), F32) for _ in range(n_stage)]
    scratch += [pltpu.VMEM((rep_rows, band_pad), BF16) for _ in range(n_stage)]
    scratch += [pltpu.VMEM((rep_rows, LANES), F32) for _ in range(n_stage)]
    return pl.pallas_call(
        functools.partial(_mixer_kernel, tile=tile, chunk=chunk, n_tiles=n_tiles, has_past=has_past,
                          alpha=alpha),
        grid=(bsz, n_tiles),
        in_specs=in_specs,
        out_specs=out_specs,
        out_shape=out_shape,
        scratch_shapes=scratch,
        compiler_params=pltpu.CompilerParams(
            dimension_semantics=("arbitrary", "arbitrary"),
            vmem_limit_bytes=VMEM_LIMIT_BYTES),
        name="mixer_past" if has_past else "mixer",
    )(*args)


def _ffn_kernel(*refs, tile, has_past, alpha):
    x1_ref = refs[0]
    n_in = 1
    if has_past:
        fp_ref = refs[1]
        n_in = 2
    w_up_ref, fcw_ref, fcb_ref, w_dn_ref, l2g_ref, l2b_ref = refs[n_in:n_in + 6]
    y_ref, fo_ref = refs[n_in + 6:n_in + 8]
    h_s, g_s = refs[n_in + 8:]
    t = pl.program_id(1)
    hist = FFN_CONV - 1

    @pl.when(t == 0)
    def _init():
        h_s[0:TAIL_ROWS] = jnp.zeros((TAIL_ROWS, 2 * D_FF), F32)
        if has_past:
            h_s[TAIL_ROWS - hist:TAIL_ROWS] = fp_ref[0]

    x1 = x1_ref[0]
    h_s[TAIL_ROWS:TAIL_ROWS + tile] = _dot(x1.astype(BF16), w_up_ref[...])

    base = TAIL_ROWS - hist

    def conv(cs):
        acc = fcb_ref[:, cs] + fcw_ref[0:1, cs] * h_s[base:base + tile, cs]
        for tap in range(1, FFN_CONV):
            acc = acc + fcw_ref[tap:tap + 1, cs] * h_s[base + tap:base + tap + tile, cs]
        return acc

    for cblk in range(D_FF // LANES):
        cs_v = slice(cblk * LANES, (cblk + 1) * LANES)
        cs_g = slice(D_FF + cblk * LANES, D_FF + (cblk + 1) * LANES)
        g_s[:, cs_v] = (conv(cs_v) * _silu(conv(cs_g))).astype(BF16)

    new_tail = h_s[TAIL_ROWS + tile - hist:TAIL_ROWS + tile]
    fo_ref[0] = new_tail
    h_s[TAIL_ROWS - hist:TAIL_ROWS] = new_tail

    f = _dot(g_s[...], w_dn_ref[...])
    y_ref[0] = _layer_norm(alpha * x1 + f, l2g_ref[...], l2b_ref[...])


def _conv_ffn(x1, ffn_past, weights, *, tile, alpha):
    bsz, seq, _ = x1.shape
    assert seq % tile == 0
    has_past = ffn_past is not None
    in_specs = [pl.BlockSpec((1, tile, D_MODEL), lambda b, t: (b, t, 0))]
    args = [x1]
    if has_past:
        in_specs.append(pl.BlockSpec((1, FFN_CONV - 1, 2 * D_FF), lambda b, t: (b, 0, 0)))
        args.append(ffn_past)
    for w in weights:
        in_specs.append(_const_spec(w.shape))
        args.append(w)
    out_shape = (
        jax.ShapeDtypeStruct((bsz, seq, D_MODEL), F32),
        jax.ShapeDtypeStruct((bsz, FFN_CONV - 1, 2 * D_FF), F32),
    )
    out_specs = (
        pl.BlockSpec((1, tile, D_MODEL), lambda b, t: (b, t, 0)),
        pl.BlockSpec((1, FFN_CONV - 1, 2 * D_FF), lambda b, t: (b, 0, 0)),
    )
    scratch = [
        pltpu.VMEM((TAIL_ROWS + tile, 2 * D_FF), F32),
        pltpu.VMEM((tile, D_FF), BF16),
    ]
    return pl.pallas_call(
        functools.partial(_ffn_kernel, tile=tile, has_past=has_past, alpha=alpha),
        grid=(bsz, seq // tile),
        in_specs=in_specs,
        out_specs=out_specs,
        out_shape=out_shape,
        scratch_shapes=scratch,
        compiler_params=pltpu.CompilerParams(
            dimension_semantics=("arbitrary", "arbitrary"),
            vmem_limit_bytes=VMEM_LIMIT_BYTES),
        name="conv_ffn_past" if has_past else "conv_ffn",
    )(*args)


def _pad_lanes(vec):
    return jnp.pad(vec.astype(F32), (0, LANES - vec.shape[0]))[None, :]


def _row(vec):
    return vec.astype(F32)[None, :]


def kernel(x_prompt, x_sample, cache_attn_k, cache_attn_v, state_ssm, state_ssm_conv, state_ffn_conv,
           w_in, rel_bias, attn_norm_g, ssm_conv_w, ssm_conv_b, ssm_dt_bias, ssm_A_log, ssm_D,
           ssm_norm_g, w_out, ln1_g, ln1_b, w_up, ffn_conv_w, ffn_conv_b, w_down, ln2_g, ln2_b):
    depth = w_in.shape[0]
    alpha = (2.0 * depth) ** 0.25
    n_dec, dec_seq, _ = x_sample.shape
    n_cache = cache_attn_k.shape[2]
    assert n_cache == WINDOW

    yp, ys = x_prompt, x_sample
    outs_p, outs_s = [], []
    for l in range(depth):
        w_in_p = jnp.pad(w_in[l], ((0, 0), (0, W_IN_COLS - w_in.shape[2]))).astype(BF16)
        bias = _bias_table(rel_bias[l])

        def grouped_bias(chunk):
            band = WINDOW + chunk
            grouped = bias[:, :chunk, :band].reshape(ATTN_GROUPS, HEADS_PER_ATTN_GROUP * chunk, band)
            return jnp.pad(grouped, ((0, 0), (0, 0), (0, _round_up(band, LANES) - band)))

        mixer_w = (
            w_in_p,
            None,
            _row(attn_norm_g[l]),
            ssm_conv_w[l].astype(F32),
            _row(ssm_conv_b[l]),
            _pad_lanes(ssm_dt_bias[l]),
            _pad_lanes(ssm_A_log[l]),
            _row(jnp.repeat(ssm_D[l], SSM_HEAD_DIM)),
            _row(ssm_norm_g[l]),
            w_out[l].astype(BF16),
            _row(ln1_g[l]),
            _row(ln1_b[l]),
        )
        ffn_w = (
            w_up[l].astype(BF16),
            ffn_conv_w[l].astype(F32),
            _row(ffn_conv_b[l]),
            w_down[l].astype(BF16),
            _row(ln2_g[l]),
            _row(ln2_b[l]),
        )
        w_prompt = (mixer_w[0], grouped_bias(CHUNK)) + mixer_w[2:]
        w_sample = (mixer_w[0], grouped_bias(dec_seq)) + mixer_w[2:]
        x1, kp, vp, hp, cp = _mixer(yp, None, w_prompt, tile=WINDOW, chunk=CHUNK, alpha=alpha)
        yp, fp = _conv_ffn(x1, None, ffn_w, tile=256, alpha=alpha)
        outs_p.append((kp, vp, hp, cp, fp))
        past = (
            cache_attn_k[l].reshape(n_dec, n_cache, D_ATTN),
            cache_attn_v[l].reshape(n_dec, n_cache, D_ATTN),
            state_ssm[l].reshape(n_dec, SSM_GROUPS, HEADS_PER_GROUP * SSM_HEAD_DIM, SSM_STATE),
            state_ssm_conv[l],
        )
        x1, ks, vs, hs, cs = _mixer(ys, past, w_sample, tile=dec_seq, chunk=dec_seq, alpha=alpha)
        ys, fs = _conv_ffn(x1, state_ffn_conv[l], ffn_w, tile=dec_seq, alpha=alpha)
        outs_s.append((ks, vs, hs, cs, fs))

    def stack(outs, idx, shape_tail):
        return jnp.stack([o[idx].reshape((o[idx].shape[0],) + shape_tail) for o in outs])

    kv_p = (min(WINDOW, x_prompt.shape[1]), N_HEADS, HEAD_DIM)
    kv_s = (dec_seq, N_HEADS, HEAD_DIM)
    st = (SSM_HEADS, SSM_HEAD_DIM, SSM_STATE)
    return (
        yp, ys,
        stack(outs_p, 0, kv_p), stack(outs_p, 1, kv_p),
        stack(outs_s, 0, kv_s), stack(outs_s, 1, kv_s),
        stack(outs_p, 2, st), stack(outs_s, 2, st),
        stack(outs_p, 3, (SSM_CONV - 1, SSM_CONV_DIM)), stack(outs_s, 3, (SSM_CONV - 1, SSM_CONV_DIM)),
        stack(outs_p, 4, (FFN_CONV - 1, 2 * D_FF)), stack(outs_s, 4, (FFN_CONV - 1, 2 * D_FF)),
    )
```

```python
import functools

import jax
import jax.numpy as jnp
from jax import lax
from jax.experimental import pallas as pl
from jax.experimental.pallas import tpu as pltpu

F32 = jnp.float32
BF16 = jnp.bfloat16

D_MODEL = 1024
CHUNK = 64
WINDOW = 512
HEAD_DIM = 64
N_HEADS = 8
D_ATTN = N_HEADS * HEAD_DIM
ATTN_SCALE = HEAD_DIM ** -0.5
LOG2E = 1.4426950408889634
HEADS_PER_ATTN_GROUP = 4
ATTN_GROUPS = N_HEADS // HEADS_PER_ATTN_GROUP
GROUP_LANES = HEADS_PER_ATTN_GROUP * HEAD_DIM
SOFTMAX_ROWS = 32
KEY_TILE = 256
FFN_COL_BLOCK = 256
MAX_REL = 128
N_REL = 2 * MAX_REL + 1
NEG_INF = -1e30
SSM_HEADS = 8
SSM_GROUPS = 2
HEADS_PER_GROUP = SSM_HEADS // SSM_GROUPS
SSM_HEAD_DIM = 64
SSM_STATE = 128
D_SSM = SSM_HEADS * SSM_HEAD_DIM
SSM_CONV = 4
SSM_CONV_DIM = D_SSM + 2 * SSM_GROUPS * SSM_STATE
D_FF = 2688
FFN_CONV = 3
LN_EPS = 1e-5
RMS_EPS = 1e-5

LANES = 128
SUBLANES = 8
VMEM_LIMIT_BYTES = 56 * 1024 * 1024

COL_Q = 0
COL_K = D_ATTN
COL_V = 2 * D_ATTN
COL_Z = 3 * D_ATTN
COL_XBC = COL_Z + D_SSM
COL_DT = COL_XBC + SSM_CONV_DIM
W_IN_COLS = COL_DT + LANES
TAIL_ROWS = SUBLANES


VECTOR_ROWS = 64
NORM_ROWS = 32


def _round_up(n, m):
    return -(-n // m) * m


def _row_blocks(n, blk):
    return [(r, min(r + blk, n)) for r in range(0, n, blk)]


def _nt_dot(a, b):
    return lax.dot_general(a, b, (((1,), (1,)), ((), ())), preferred_element_type=F32)


def _tn_dot(a, b):
    return lax.dot_general(a, b, (((0,), (0,)), ((), ())), preferred_element_type=F32)


def _dot(a, b):
    return jnp.dot(a, b, preferred_element_type=F32)


def _silu(x):
    return x * jax.nn.sigmoid(x)


def _softplus(x):
    return jnp.maximum(x, 0.0) + jnp.log1p(jnp.exp(-jnp.abs(x)))


def _layer_norm(u, g, b):
    mu = jnp.mean(u, axis=-1, keepdims=True)
    d = u - mu
    var = jnp.mean(d * d, axis=-1, keepdims=True)
    return d * lax.rsqrt(var + LN_EPS) * g + b


def _rms_norm(u, g):
    return u * lax.rsqrt(jnp.mean(u * u, axis=-1, keepdims=True) + RMS_EPS) * g


def _bias_kernel(rb_ref, out_ref):
    band = WINDOW + CHUNK
    first = max(0, WINDOW - MAX_REL) // LANES * LANES
    i = lax.broadcasted_iota(jnp.int32, (CHUNK, band - first), 0)
    j = lax.broadcasted_iota(jnp.int32, (CHUNK, band - first), 1) + first
    idx = jnp.clip(WINDOW + i - j, -MAX_REL, MAX_REL) + MAX_REL
    idx_min = max(-MAX_REL, WINDOW - (band - 1)) + MAX_REL
    for h in range(N_HEADS):
        def body(r, acc, h=h):
            return jnp.where(idx == r, rb_ref[h, r] * LOG2E, acc)
        far = rb_ref[h, N_REL - 1] * LOG2E
        if first > 0:
            out_ref[h, :, 0:first] = jnp.full((CHUNK, first), far, F32)
        out_ref[h, :, first:band] = lax.fori_loop(idx_min, N_REL - 1, body,
                                                  jnp.full((CHUNK, band - first), far, F32))


def _bias_table(rel_bias):
    return pl.pallas_call(
        _bias_kernel,
        out_shape=jax.ShapeDtypeStruct((N_HEADS, CHUNK, WINDOW + CHUNK), F32),
        in_specs=[pl.BlockSpec(memory_space=pltpu.SMEM)],
        out_specs=pl.BlockSpec(memory_space=pltpu.VMEM),
        name="rel_bias_table",
    )(rel_bias)


def _mixer_kernel(*refs, tile, chunk, n_tiles, has_past, alpha):
    n_state_in = 4 if has_past else 0
    x_ref = refs[0]
    if has_past:
        kp_ref, vp_ref, h0_ref, cp_ref = refs[1:5]
    (w_in_ref, bias_ref, ag_ref, cw_ref, cb_ref, dtb_ref, alog_ref, dsk_ref, sg_ref, w_out_ref,
     l1g_ref, l1b_ref) = refs[1 + n_state_in:13 + n_state_in]
    x1_ref, ko_ref, vo_ref, ho_ref, co_ref = refs[13 + n_state_in:18 + n_state_in]
    (kwin, vwin, q_s, qrep_s, z_s, xp_s, xc_s, a_s, y_s, dt_s, ccol_s, crow_s, drow_s, ht_s) = refs[
        18 + n_state_in:32 + n_state_in]
    stage = refs[32 + n_state_in:]
    s_bufs = (stage[0:2], stage[2:4])
    m_bufs = (stage[4:6], stage[6:8])
    p_bufs = (stage[8:10], stage[10:12])
    lp_bufs = (stage[12:14], stage[14:16])

    n_chunks = tile // chunk
    band = WINDOW + chunk
    band_pad = _round_up(band, LANES)
    t = pl.program_id(1)

    @pl.when(t == 0)
    def _init():
        kwin[WINDOW + tile:] = jnp.zeros((band_pad - band, D_ATTN), BF16)
        vwin[WINDOW + tile:] = jnp.zeros((band_pad - band, D_ATTN), BF16)
        if has_past:
            kwin[0:WINDOW] = kp_ref[0].astype(BF16)
            vwin[0:WINDOW] = vp_ref[0].astype(BF16)
            for g in range(SSM_GROUPS):
                ht_s[g] = h0_ref[0, g].T
            xp_s[0:TAIL_ROWS] = jnp.zeros((TAIL_ROWS, SSM_CONV_DIM), F32)
            xp_s[TAIL_ROWS - (SSM_CONV - 1):TAIL_ROWS] = cp_ref[0]
        else:
            kwin[0:WINDOW] = jnp.zeros((WINDOW, D_ATTN), BF16)
            vwin[0:WINDOW] = jnp.zeros((WINDOW, D_ATTN), BF16)
            ht_s[...] = jnp.zeros(ht_s.shape, F32)
            xp_s[0:TAIL_ROWS] = jnp.zeros((TAIL_ROWS, SSM_CONV_DIM), F32)

    conv_hist = SSM_CONV - 1
    if n_tiles > 1:
        @pl.when(t > 0)
        def _carry_history():
            kwin[0:WINDOW] = kwin[tile:tile + WINDOW]
            vwin[0:WINDOW] = vwin[tile:tile + WINDOW]
            xp_s[TAIL_ROWS - conv_hist:TAIL_ROWS] = xp_s[TAIL_ROWS + tile - conv_hist:TAIL_ROWS + tile]

    def interleave(stages):
        stages = list(stages)
        while stages:
            for stage in list(stages):
                try:
                    next(stage)
                except StopIteration:
                    stages.remove(stage)

    x = x_ref[0]
    xb = x.astype(BF16)

    def proj(lo, hi):
        return _dot(xb, w_in_ref[:, lo:hi])

    xp_s[TAIL_ROWS:TAIL_ROWS + tile] = proj(COL_XBC, COL_DT)
    dt_raw = proj(COL_DT, W_IN_COLS)
    co_ref[0] = xp_s[TAIL_ROWS + tile - conv_hist:TAIL_ROWS + tile]

    def attention_projections():
        head_of_lane = lax.broadcasted_iota(jnp.int32, (chunk, GROUP_LANES), 1) // HEAD_DIM
        for g in range(ATTN_GROUPS):
            gs = slice(g * GROUP_LANES, (g + 1) * GROUP_LANES)
            q_s[:, gs] = proj(COL_Q + g * GROUP_LANES, COL_Q + (g + 1) * GROUP_LANES) * (ATTN_SCALE * LOG2E)
            yield
            for c in range(n_chunks):
                q_cg = q_s[c * chunk:(c + 1) * chunk, gs]
                qrep_s[c, g] = jnp.concatenate(
                    [jnp.where(head_of_lane == h, q_cg, 0.0) for h in range(HEADS_PER_ATTN_GROUP)],
                    axis=0).astype(BF16)
        for col, out_ref, win in ((COL_K, ko_ref, kwin), (COL_V, vo_ref, vwin)):
            for g in range(ATTN_GROUPS):
                gs = slice(g * GROUP_LANES, (g + 1) * GROUP_LANES)
                kv = proj(col + g * GROUP_LANES, col + (g + 1) * GROUP_LANES)
                out_ref[0, :, gs] = kv
                win[WINDOW:WINDOW + tile, gs] = kv.astype(BF16)
                yield
        for g in range(D_SSM // GROUP_LANES):
            gs = slice(g * GROUP_LANES, (g + 1) * GROUP_LANES)
            z_s[:, gs] = proj(COL_Z + g * GROUP_LANES, COL_Z + (g + 1) * GROUP_LANES)
            yield

    def conv_silu():
        base = TAIL_ROWS - conv_hist
        for cblk in range(SSM_CONV_DIM // LANES):
            cs = slice(cblk * LANES, (cblk + 1) * LANES)
            acc = cb_ref[:, cs] + cw_ref[0:1, cs] * xp_s[base:base + tile, cs]
            for tap in range(1, SSM_CONV):
                acc = acc + cw_ref[tap:tap + 1, cs] * xp_s[base + tap:base + tap + tile, cs]
            xc_s[:, cs] = _silu(acc)
            yield

    def to_row_form(blk):
        if chunk < LANES:
            blk = jnp.concatenate([blk, jnp.zeros((LANES - chunk, LANES), F32)], axis=0)
        return blk.T[0:SUBLANES]

    def decay_tables():
        dt = _softplus(dt_raw + dtb_ref[...])
        dt_s[...] = dt
        a = dt * (-jnp.exp(alog_ref[...]))
        pad_rows = ccol_s.shape[0]
        if pad_rows > tile:
            a = jnp.concatenate([a, jnp.zeros((pad_rows - tile, LANES), F32)], axis=0)
        li = lax.broadcasted_iota(jnp.int32, (LANES, LANES), 0)
        si = lax.broadcasted_iota(jnp.int32, (LANES, LANES), 1)
        shift = chunk.bit_length() - 1
        tril_bd = jnp.where((si <= li) & ((li >> shift) == (si >> shift)), 1.0, 0.0).astype(F32)
        yield
        for blk in range(pad_rows // LANES):
            rs = slice(blk * LANES, (blk + 1) * LANES)
            ccol_s[rs] = jnp.dot(tril_bd, a[rs], precision=lax.Precision.HIGHEST,
                                 preferred_element_type=F32)
            yield
        for c in range(n_chunks):
            crow_s[c] = to_row_form(ccol_s[c * chunk:(c + 1) * chunk])
            drow_s[c] = to_row_form(dt_s[c * chunk:(c + 1) * chunk])
            if c % 2 == 1:
                yield

    interleave([attention_projections(), conv_silu(), decay_tables()])

    lane = lax.broadcasted_iota(jnp.int32, (chunk, LANES), 1)
    lo_half = lane < HEAD_DIM
    tri = (lax.broadcasted_iota(jnp.int32, (chunk, chunk), 0)
           >= lax.broadcasted_iota(jnp.int32, (chunk, chunk), 1))
    band_col = lax.broadcasted_iota(jnp.int32, (1, band_pad), 1)
    state_lane = lax.broadcasted_iota(jnp.int32, (1, HEADS_PER_GROUP * SSM_HEAD_DIM), 1)

    def bcast(mat, col, width):
        return jnp.broadcast_to(mat[:, col:col + 1], (chunk, width))

    def pair(mat, h_even):
        return jnp.where(lo_half, bcast(mat, h_even, LANES), bcast(mat, h_even + 1, LANES))

    def row_start(c):
        return c * chunk if isinstance(c, int) else pl.multiple_of(c * chunk, chunk)

    rep_rows = HEADS_PER_ATTN_GROUP * chunk
    group_lane = lax.broadcasted_iota(jnp.int32, (chunk, GROUP_LANES), 1)

    def attn_scores(c, g, s_ref, m_ref):
        r0 = row_start(c)
        q_rep = qrep_s[c, g]
        m_run = None
        for lo in range(0, band_pad, KEY_TILE):
            hi = min(lo + KEY_TILE, band_pad)
            k_t = kwin[pl.ds(pl.multiple_of(r0 + lo, chunk), hi - lo),
                       g * GROUP_LANES:(g + 1) * GROUP_LANES]
            s = _nt_dot(q_rep, k_t) + bias_ref[g, :, lo:hi]
            col = band_col[:, lo:hi]
            if has_past:
                valid = col < band
            else:
                valid = (col < band) & ((t * tile + r0 - WINDOW + col) >= 0)
            s = jnp.where(valid, s, NEG_INF)
            s_ref[:, lo:hi] = s
            for j in range((hi - lo) // LANES):
                s_j = s[:, j * LANES:(j + 1) * LANES]
                m_run = s_j if m_run is None else jnp.maximum(m_run, s_j)
            yield
        m_ref[...] = jnp.broadcast_to(jnp.max(m_run, axis=-1, keepdims=True), (rep_rows, LANES))
        yield

    def attn_softmax(s_ref, m_ref, p_ref, lp_ref):
        for blk in range(rep_rows // SOFTMAX_ROWS):
            rs = slice(blk * SOFTMAX_ROWS, (blk + 1) * SOFTMAX_ROWS)
            m = m_ref[rs, :]
            p_tiles = [jnp.exp2(s_ref[rs, j * LANES:(j + 1) * LANES] - m) for j in range(band_pad // LANES)]
            lp = p_tiles[0]
            for p_j in p_tiles[1:]:
                lp = lp + p_j
            lp_ref[rs, :] = lp
            p_ref[rs, :] = jnp.concatenate(p_tiles, axis=1).astype(BF16)
            yield

    def attn_output(c, g, p_ref, lp_ref):
        r0 = row_start(c)
        inv = 1.0 / jnp.sum(lp_ref[...], axis=-1, keepdims=True)
        yield
        v_g = vwin[pl.ds(r0, band_pad), g * GROUP_LANES:(g + 1) * GROUP_LANES]
        o_all = _dot(p_ref[...], v_g) * inv
        o = o_all[(HEADS_PER_ATTN_GROUP - 1) * chunk:]
        for h in reversed(range(HEADS_PER_ATTN_GROUP - 1)):
            o = jnp.where(group_lane < (h + 1) * HEAD_DIM, o_all[h * chunk:(h + 1) * chunk], o)
        a_s[pl.ds(r0, chunk), g * GROUP_LANES:(g + 1) * GROUP_LANES] = o
        yield

    def ssd_chunk(c):
        r0 = row_start(c)
        rows = pl.ds(r0, chunk)
        dtc = dt_s[rows, :]
        ccol = ccol_s[rows, :]
        crow = crow_s[c]
        drow = drow_s[c]
        clast = ccol[chunk - 1:chunk, :]
        wcol = jnp.exp(clast - ccol) * dtc
        ecol = jnp.exp(ccol)
        cdec = jnp.exp(clast)
        yield
        for g in range(SSM_GROUPS):
            b_g = xc_s[rows, D_SSM + g * SSM_STATE:D_SSM + (g + 1) * SSM_STATE].astype(BF16)
            c_g = xc_s[rows, D_SSM + (SSM_GROUPS + g) * SSM_STATE:
                       D_SSM + (SSM_GROUPS + g + 1) * SSM_STATE].astype(BF16)
            cb = _nt_dot(c_g, b_g)
            h_t = ht_s[g]
            y_off = _dot(c_g, h_t.astype(BF16))
            xw_parts = []
            for jj in range(HEADS_PER_GROUP // 2):
                slab = g * (HEADS_PER_GROUP // 2) + jj
                h_even = HEADS_PER_GROUP * g + 2 * jj
                cs = slice(slab * LANES, (slab + 1) * LANES)
                x2 = xc_s[rows, cs]
                x2_bf = x2.astype(BF16)
                y_diag = None
                for e in range(2):
                    h = h_even + e
                    seg = bcast(ccol, h, chunk) - crow[h:h + 1, 0:chunk]
                    lmat = jnp.where(tri, jnp.exp(seg), 0.0) * drow[h:h + 1, 0:chunk]
                    yd = _dot((cb * lmat).astype(BF16), x2_bf)
                    y_diag = yd if e == 0 else jnp.where(lo_half, y_diag, yd)
                y2 = (y_diag + y_off[:, jj * LANES:(jj + 1) * LANES] * pair(ecol, h_even)
                      + x2 * dsk_ref[:, cs])
                y_s[rows, cs] = y2
                xw_parts.append((x2 * pair(wcol, h_even)).astype(BF16))
                yield
            xw = jnp.concatenate(xw_parts, axis=1)
            dec = None
            for r in reversed(range(HEADS_PER_GROUP)):
                h = HEADS_PER_GROUP * g + r
                d_r = jnp.broadcast_to(cdec[:, h:h + 1], (1, HEADS_PER_GROUP * SSM_HEAD_DIM))
                dec = d_r if dec is None else jnp.where(state_lane < (r + 1) * SSM_HEAD_DIM, d_r, dec)
            ht_s[g] = h_t * dec + _tn_dot(b_g, xw)
            yield

    def step(c, slot, scores_next, softmax_cur, output_prev):
        stages = []
        for g in range(ATTN_GROUPS):
            if scores_next:
                stages.append(attn_scores(c + 1, g, s_bufs[1 - slot][g], m_bufs[1 - slot][g]))
            if softmax_cur:
                stages.append(attn_softmax(s_bufs[slot][g], m_bufs[slot][g], p_bufs[slot][g],
                                           lp_bufs[slot][g]))
            if output_prev:
                stages.append(attn_output(c - 1, g, p_bufs[1 - slot][g], lp_bufs[1 - slot][g]))
        stages.append(ssd_chunk(c))
        interleave(stages)

    interleave(attn_scores(0, g, s_bufs[0][g], m_bufs[0][g]) for g in range(ATTN_GROUPS))
    if n_chunks == 1:
        step(0, 0, False, True, False)
        interleave(attn_output(0, g, p_bufs[0][g], lp_bufs[0][g]) for g in range(ATTN_GROUPS))
    else:
        assert n_chunks % 2 == 0
        step(0, 0, True, True, False)

        for c in range(1, n_chunks - 1):
            step(c, c % 2, True, True, True)
        step(n_chunks - 1, 1, False, True, True)
        interleave(attn_output(n_chunks - 1, g, p_bufs[1][g], lp_bufs[1][g]) for g in range(ATTN_GROUPS))

    for g in range(SSM_GROUPS):
        ho_ref[0, g] = ht_s[g].T

    s_n = _rms_norm(y_s[...] * _silu(z_s[...]), sg_ref[...]).astype(BF16)
    mix_s = _dot(s_n, w_out_ref[D_ATTN:D_ATTN + D_SSM, :])
    a_n = _rms_norm(a_s[...], ag_ref[...]).astype(BF16)
    mix = _dot(a_n, w_out_ref[0:D_ATTN, :]) + mix_s
    x1_ref[0] = _layer_norm(alpha * x + mix, l1g_ref[...], l1b_ref[...])


def _const_spec(shape):
    zeros = (0,) * len(shape)
    return pl.BlockSpec(shape, lambda b, t: zeros, pipeline_mode=pl.Buffered(1))


def _mixer(x, past, weights, *, tile, chunk, alpha):
    bsz, seq, _ = x.shape
    n_tiles = seq // tile
    assert seq % tile == 0 and tile % chunk == 0
    has_past = past is not None
    keep = min(WINDOW, seq)
    assert keep == tile, "the kept K/V rows must be exactly the last tile"
    pad_rows = max(tile, LANES)
    n_chunks = tile // chunk
    rep_rows = HEADS_PER_ATTN_GROUP * chunk
    band = WINDOW + chunk
    band_pad = _round_up(band, LANES)

    def per_stream(shape):
        zeros = (0,) * (len(shape) - 1)
        return pl.BlockSpec((1,) + tuple(shape[1:]), lambda b, t: (b,) + zeros)

    in_specs = [pl.BlockSpec((1, tile, D_MODEL), lambda b, t: (b, t, 0))]
    args = [x]
    if has_past:
        for arr in past:
            in_specs.append(per_stream(arr.shape))
            args.append(arr)
    for w in weights:
        in_specs.append(_const_spec(w.shape))
        args.append(w)

    out_shape = (
        jax.ShapeDtypeStruct((bsz, seq, D_MODEL), F32),
        jax.ShapeDtypeStruct((bsz, keep, D_ATTN), F32),
        jax.ShapeDtypeStruct((bsz, keep, D_ATTN), F32),
        jax.ShapeDtypeStruct((bsz, SSM_GROUPS, HEADS_PER_GROUP * SSM_HEAD_DIM, SSM_STATE), F32),
        jax.ShapeDtypeStruct((bsz, SSM_CONV - 1, SSM_CONV_DIM), F32),
    )
    out_specs = (
        pl.BlockSpec((1, tile, D_MODEL), lambda b, t: (b, t, 0)),
        per_stream(out_shape[1].shape),
        per_stream(out_shape[2].shape),
        per_stream(out_shape[3].shape),
        per_stream(out_shape[4].shape),
    )
    win_rows = WINDOW + tile + band_pad - band
    scratch = [
        pltpu.VMEM((win_rows, D_ATTN), BF16),
        pltpu.VMEM((win_rows, D_ATTN), BF16),
        pltpu.VMEM((tile, D_ATTN), F32),
        pltpu.VMEM((n_chunks, ATTN_GROUPS, rep_rows, GROUP_LANES), BF16),
        pltpu.VMEM((tile, D_SSM), F32),
        pltpu.VMEM((TAIL_ROWS + tile, SSM_CONV_DIM), F32),
        pltpu.VMEM((tile, SSM_CONV_DIM), F32),
        pltpu.VMEM((tile, D_ATTN), F32),
        pltpu.VMEM((tile, D_SSM), F32),
        pltpu.VMEM((tile, LANES), F32),
        pltpu.VMEM((pad_rows, LANES), F32),
        pltpu.VMEM((n_chunks, SUBLANES, LANES), F32),
        pltpu.VMEM((n_chunks, SUBLANES, LANES), F32),
        pltpu.VMEM((SSM_GROUPS, SSM_STATE, HEADS_PER_GROUP * SSM_HEAD_DIM), F32),
    ]
    n_stage = 2 * ATTN_GROUPS
    scratch += [pltpu.VMEM((rep_rows, band_pad), F32) for _ in range(n_stage)]
    scratch += [pltpu.VMEM((rep_rows, LANES), F32) for _ in range(n_stage)]
    scratch += [pltpu.VMEM((rep_rows, band_pad), BF16) for _ in range(n_stage)]
    scratch += [pltpu.VMEM((rep_rows, LANES), F32) for _ in range(n_stage)]
    return pl.pallas_call(
        functools.partial(_mixer_kernel, tile=tile, chunk=chunk, n_tiles=n_tiles, has_past=has_past,
                          alpha=alpha),
        grid=(bsz, n_tiles),
        in_specs=in_specs,
        out_specs=out_specs,
        out_shape=out_shape,
        scratch_shapes=scratch,
        compiler_params=pltpu.CompilerParams(
            dimension_semantics=("arbitrary", "arbitrary"),
            vmem_limit_bytes=VMEM_LIMIT_BYTES),
        name="mixer_past" if has_past else "mixer",
    )(*args)


def _ffn_kernel(*refs, tile, has_past, alpha):
    x1_ref = refs[0]
    n_in = 1
    if has_past:
        fp_ref = refs[1]
        n_in = 2
    w_up_ref, fcw_ref, fcb_ref, w_dn_ref, l2g_ref, l2b_ref = refs[n_in:n_in + 6]
    y_ref, fo_ref = refs[n_in + 6:n_in + 8]
    h_s, g_s = refs[n_in + 8:]
    t = pl.program_id(1)
    hist = FFN_CONV - 1

    @pl.when(t == 0)
    def _init():
        h_s[0:TAIL_ROWS] = jnp.zeros((TAIL_ROWS, 2 * D_FF), F32)
        if has_past:
            h_s[TAIL_ROWS - hist:TAIL_ROWS] = fp_ref[0]

    @pl.when(t > 0)
    def _carry_history():
        h_s[TAIL_ROWS - hist:TAIL_ROWS] = h_s[TAIL_ROWS + tile - hist:TAIL_ROWS + tile]

    x1 = x1_ref[0]
    h_s[TAIL_ROWS:TAIL_ROWS + tile] = _dot(x1.astype(BF16), w_up_ref[...])
    fo_ref[0] = h_s[TAIL_ROWS + tile - hist:TAIL_ROWS + tile]

    base = TAIL_ROWS - hist

    def conv(cs):
        acc = fcb_ref[:, cs] + fcw_ref[0:1, cs] * h_s[base:base + tile, cs]
        for tap in range(1, FFN_CONV):
            acc = acc + fcw_ref[tap:tap + 1, cs] * h_s[base + tap:base + tap + tile, cs]
        return acc

    for cblk in range(D_FF // LANES):
        cs_v = slice(cblk * LANES, (cblk + 1) * LANES)
        cs_g = slice(D_FF + cblk * LANES, D_FF + (cblk + 1) * LANES)
        g_s[:, cs_v] = (conv(cs_v) * _silu(conv(cs_g))).astype(BF16)

    f = _dot(g_s[...], w_dn_ref[...])
    y_ref[0] = _layer_norm(alpha * x1 + f, l2g_ref[...], l2b_ref[...])


def _conv_ffn(x1, ffn_past, weights, *, tile, alpha):
    bsz, seq, _ = x1.shape
    assert seq % tile == 0
    has_past = ffn_past is not None
    in_specs = [pl.BlockSpec((1, tile, D_MODEL), lambda b, t: (b, t, 0))]
    args = [x1]
    if has_past:
        in_specs.append(pl.BlockSpec((1, FFN_CONV - 1, 2 * D_FF), lambda b, t: (b, 0, 0)))
        args.append(ffn_past)
    for w in weights:
        in_specs.append(_const_spec(w.shape))
        args.append(w)
    out_shape = (
        jax.ShapeDtypeStruct((bsz, seq, D_MODEL), F32),
        jax.ShapeDtypeStruct((bsz, FFN_CONV - 1, 2 * D_FF), F32),
    )
    out_specs = (
        pl.BlockSpec((1, tile, D_MODEL), lambda b, t: (b, t, 0)),
        pl.BlockSpec((1, FFN_CONV - 1, 2 * D_FF), lambda b, t: (b, 0, 0)),
    )
    scratch = [
        pltpu.VMEM((TAIL_ROWS + tile, 2 * D_FF), F32),
        pltpu.VMEM((tile, D_FF), BF16),
    ]
    return pl.pallas_call(
        functools.partial(_ffn_kernel, tile=tile, has_past=has_past, alpha=alpha),
        grid=(bsz, seq // tile),
        in_specs=in_specs,
        out_specs=out_specs,
        out_shape=out_shape,
        scratch_shapes=scratch,
        compiler_params=pltpu.CompilerParams(
            dimension_semantics=("arbitrary", "arbitrary"),
            vmem_limit_bytes=VMEM_LIMIT_BYTES),
        name="conv_ffn_past" if has_past else "conv_ffn",
    )(*args)


def _pad_lanes(vec):
    return jnp.pad(vec.astype(F32), (0, LANES - vec.shape[0]))[None, :]


def _row(vec):
    return vec.astype(F32)[None, :]


def kernel(x_prompt, x_sample, cache_attn_k, cache_attn_v, state_ssm, state_ssm_conv, state_ffn_conv,
           w_in, rel_bias, attn_norm_g, ssm_conv_w, ssm_conv_b, ssm_dt_bias, ssm_A_log, ssm_D,
           ssm_norm_g, w_out, ln1_g, ln1_b, w_up, ffn_conv_w, ffn_conv_b, w_down, ln2_g, ln2_b):
    depth = w_in.shape[0]
    alpha = (2.0 * depth) ** 0.25
    n_dec, dec_seq, _ = x_sample.shape
    n_cache = cache_attn_k.shape[2]
    assert n_cache == WINDOW

    yp, ys = x_prompt, x_sample
    outs_p, outs_s = [], []
    for l in range(depth):
        w_in_p = jnp.pad(w_in[l], ((0, 0), (0, W_IN_COLS - w_in.shape[2]))).astype(BF16)
        bias = _bias_table(rel_bias[l])

        def grouped_bias(chunk):
            band = WINDOW + chunk
            grouped = bias[:, :chunk, :band].reshape(ATTN_GROUPS, HEADS_PER_ATTN_GROUP * chunk, band)
            return jnp.pad(grouped, ((0, 0), (0, 0), (0, _round_up(band, LANES) - band)))

        mixer_w = (
            w_in_p,
            None,
            _row(attn_norm_g[l]),
            ssm_conv_w[l].astype(F32),
            _row(ssm_conv_b[l]),
            _pad_lanes(ssm_dt_bias[l]),
            _pad_lanes(ssm_A_log[l]),
            _row(jnp.repeat(ssm_D[l], SSM_HEAD_DIM)),
            _row(ssm_norm_g[l]),
            w_out[l].astype(BF16),
            _row(ln1_g[l]),
            _row(ln1_b[l]),
        )
        ffn_w = (
            w_up[l].astype(BF16),
            ffn_conv_w[l].astype(F32),
            _row(ffn_conv_b[l]),
            w_down[l].astype(BF16),
            _row(ln2_g[l]),
            _row(ln2_b[l]),
        )
        w_prompt = (mixer_w[0], grouped_bias(CHUNK)) + mixer_w[2:]
        w_sample = (mixer_w[0], grouped_bias(dec_seq)) + mixer_w[2:]
        x1, kp, vp, hp, cp = _mixer(yp, None, w_prompt, tile=WINDOW, chunk=CHUNK, alpha=alpha)
        yp, fp = _conv_ffn(x1, None, ffn_w, tile=256, alpha=alpha)
        outs_p.append((kp, vp, hp, cp, fp))
        past = (
            cache_attn_k[l].reshape(n_dec, n_cache, D_ATTN),
            cache_attn_v[l].reshape(n_dec, n_cache, D_ATTN),
            state_ssm[l].reshape(n_dec, SSM_GROUPS, HEADS_PER_GROUP * SSM_HEAD_DIM, SSM_STATE),
            state_ssm_conv[l],
        )
        x1, ks, vs, hs, cs = _mixer(ys, past, w_sample, tile=dec_seq, chunk=dec_seq, alpha=alpha)
        ys, fs = _conv_ffn(x1, state_ffn_conv[l], ffn_w, tile=dec_seq, alpha=alpha)
        outs_s.append((ks, vs, hs, cs, fs))

    def stack(outs, idx, shape_tail):
        return jnp.stack([o[idx].reshape((o[idx].shape[0],) + shape_tail) for o in outs])

    kv_p = (min(WINDOW, x_prompt.shape[1]), N_HEADS, HEAD_DIM)
    kv_s = (dec_seq, N_HEADS, HEAD_DIM)
    st = (SSM_HEADS, SSM_HEAD_DIM, SSM_STATE)
    return (
        yp, ys,
        stack(outs_p, 0, kv_p), stack(outs_p, 1, kv_p),
        stack(outs_s, 0, kv_s), stack(outs_s, 1, kv_s),
        stack(outs_p, 2, st), stack(outs_s, 2, st),
        stack(outs_p, 3, (SSM_CONV - 1, SSM_CONV_DIM)), stack(outs_s, 3, (SSM_CONV - 1, SSM_CONV_DIM)),
        stack(outs_p, 4, (FFN_CONV - 1, 2 * D_FF)), stack(outs_s, 4, (FFN_CONV - 1, 2 * D_FF)),
    )
```

```python
import functools

import jax
import jax.numpy as jnp
from jax import lax
from jax.experimental import pallas as pl
from jax.experimental.pallas import tpu as pltpu

F32 = jnp.float32
BF16 = jnp.bfloat16

D_MODEL = 1024
CHUNK = 64
WINDOW = 512
HEAD_DIM = 64
N_HEADS = 8
D_ATTN = N_HEADS * HEAD_DIM
ATTN_SCALE = HEAD_DIM ** -0.5
LOG2E = 1.4426950408889634
HEADS_PER_ATTN_GROUP = 4
ATTN_GROUPS = N_HEADS // HEADS_PER_ATTN_GROUP
GROUP_LANES = HEADS_PER_ATTN_GROUP * HEAD_DIM
SOFTMAX_ROWS = 32
KEY_TILE = 256
FFN_TILE = 512
MAX_REL = 128
N_REL = 2 * MAX_REL + 1
NEG_INF = -1e30
SSM_HEADS = 8
SSM_GROUPS = 2
HEADS_PER_GROUP = SSM_HEADS // SSM_GROUPS
SSM_HEAD_DIM = 64
SSM_STATE = 128
D_SSM = SSM_HEADS * SSM_HEAD_DIM
SSM_CONV = 4
SSM_CONV_DIM = D_SSM + 2 * SSM_GROUPS * SSM_STATE
D_FF = 2688
FFN_CONV = 3
LN_EPS = 1e-5
RMS_EPS = 1e-5

LANES = 128
SUBLANES = 8
VMEM_LIMIT_BYTES = 56 * 1024 * 1024

COL_Q = 0
COL_K = D_ATTN
COL_V = 2 * D_ATTN
COL_Z = 3 * D_ATTN
COL_XBC = COL_Z + D_SSM
COL_DT = COL_XBC + SSM_CONV_DIM
W_IN_COLS = COL_DT + LANES
TAIL_ROWS = SUBLANES


VECTOR_ROWS = 64
NORM_ROWS = 32


def _round_up(n, m):
    return -(-n // m) * m


def _row_blocks(n, blk):
    return [(r, min(r + blk, n)) for r in range(0, n, blk)]


def _nt_dot(a, b):
    return lax.dot_general(a, b, (((1,), (1,)), ((), ())), preferred_element_type=F32)


def _tn_dot(a, b):
    return lax.dot_general(a, b, (((0,), (0,)), ((), ())), preferred_element_type=F32)


def _dot(a, b):
    return jnp.dot(a, b, preferred_element_type=F32)


def _silu(x):
    return x * jax.nn.sigmoid(x)


def _softplus(x):
    return jnp.maximum(x, 0.0) + jnp.log1p(jnp.exp(-jnp.abs(x)))


def _layer_norm(u, g, b):
    mu = jnp.mean(u, axis=-1, keepdims=True)
    d = u - mu
    var = jnp.mean(d * d, axis=-1, keepdims=True)
    return d * lax.rsqrt(var + LN_EPS) * g + b


def _rms_norm(u, g):
    return u * lax.rsqrt(jnp.mean(u * u, axis=-1, keepdims=True) + RMS_EPS) * g


def _bias_kernel(rb_ref, out_ref):
    band = WINDOW + CHUNK
    first = max(0, WINDOW - MAX_REL) // LANES * LANES
    i = lax.broadcasted_iota(jnp.int32, (CHUNK, band - first), 0)
    j = lax.broadcasted_iota(jnp.int32, (CHUNK, band - first), 1) + first
    idx = jnp.clip(WINDOW + i - j, -MAX_REL, MAX_REL) + MAX_REL
    idx_min = max(-MAX_REL, WINDOW - (band - 1)) + MAX_REL
    for h in range(N_HEADS):
        def body(r, acc, h=h):
            return jnp.where(idx == r, rb_ref[h, r] * LOG2E, acc)
        far = rb_ref[h, N_REL - 1] * LOG2E
        if first > 0:
            out_ref[h, :, 0:first] = jnp.full((CHUNK, first), far, F32)
        out_ref[h, :, first:band] = lax.fori_loop(idx_min, N_REL - 1, body,
                                                  jnp.full((CHUNK, band - first), far, F32))


def _bias_table(rel_bias):
    return pl.pallas_call(
        _bias_kernel,
        out_shape=jax.ShapeDtypeStruct((N_HEADS, CHUNK, WINDOW + CHUNK), F32),
        in_specs=[pl.BlockSpec(memory_space=pltpu.SMEM)],
        out_specs=pl.BlockSpec(memory_space=pltpu.VMEM),
        name="rel_bias_table",
    )(rel_bias)


def _mixer_kernel(*refs, tile, chunk, n_tiles, has_past, alpha):
    n_state_in = 4 if has_past else 0
    x_ref = refs[0]
    if has_past:
        kp_ref, vp_ref, h0_ref, cp_ref = refs[1:5]
    (w_in_ref, bias_ref, ag_ref, cw_ref, cb_ref, dtb_ref, alog_ref, dsk_ref, sg_ref, w_out_ref,
     l1g_ref, l1b_ref) = refs[1 + n_state_in:13 + n_state_in]
    x1_ref, ko_ref, vo_ref, ho_ref, co_ref = refs[13 + n_state_in:18 + n_state_in]
    (kwin, vwin, q_s, qrep_s, z_s, xp_s, xc_s, a_s, y_s, dt_s, ccol_s, crow_s, drow_s, ht_s) = refs[
        18 + n_state_in:32 + n_state_in]
    stage = refs[32 + n_state_in:]
    s_bufs = (stage[0:2], stage[2:4])
    m_bufs = (stage[4:6], stage[6:8])
    p_bufs = (stage[8:10], stage[10:12])
    lp_bufs = (stage[12:14], stage[14:16])

    n_chunks = tile // chunk
    band = WINDOW + chunk
    band_pad = _round_up(band, LANES)
    t = pl.program_id(1)

    @pl.when(t == 0)
    def _init():
        kwin[WINDOW + tile:] = jnp.zeros((band_pad - band, D_ATTN), BF16)
        vwin[WINDOW + tile:] = jnp.zeros((band_pad - band, D_ATTN), BF16)
        if has_past:
            kwin[0:WINDOW] = kp_ref[0].astype(BF16)
            vwin[0:WINDOW] = vp_ref[0].astype(BF16)
            for g in range(SSM_GROUPS):
                ht_s[g] = h0_ref[0, g].T
            xp_s[0:TAIL_ROWS] = jnp.zeros((TAIL_ROWS, SSM_CONV_DIM), F32)
            xp_s[TAIL_ROWS - (SSM_CONV - 1):TAIL_ROWS] = cp_ref[0]
        else:
            kwin[0:WINDOW] = jnp.zeros((WINDOW, D_ATTN), BF16)
            vwin[0:WINDOW] = jnp.zeros((WINDOW, D_ATTN), BF16)
            ht_s[...] = jnp.zeros(ht_s.shape, F32)
            xp_s[0:TAIL_ROWS] = jnp.zeros((TAIL_ROWS, SSM_CONV_DIM), F32)

    conv_hist = SSM_CONV - 1
    if n_tiles > 1:
        @pl.when(t > 0)
        def _carry_history():
            kwin[0:WINDOW] = kwin[tile:tile + WINDOW]
            vwin[0:WINDOW] = vwin[tile:tile + WINDOW]
            xp_s[TAIL_ROWS - conv_hist:TAIL_ROWS] = xp_s[TAIL_ROWS + tile - conv_hist:TAIL_ROWS + tile]

    def interleave(stages):
        stages = list(stages)
        while stages:
            for stage in list(stages):
                try:
                    next(stage)
                except StopIteration:
                    stages.remove(stage)

    x = x_ref[0]
    xb = x.astype(BF16)

    def proj(lo, hi):
        return _dot(xb, w_in_ref[:, lo:hi])

    xp_s[TAIL_ROWS:TAIL_ROWS + tile] = proj(COL_XBC, COL_DT)
    dt_raw = proj(COL_DT, W_IN_COLS)
    co_ref[0] = xp_s[TAIL_ROWS + tile - conv_hist:TAIL_ROWS + tile]

    def attention_projections():
        head_of_lane = lax.broadcasted_iota(jnp.int32, (chunk, GROUP_LANES), 1) // HEAD_DIM
        for g in range(ATTN_GROUPS):
            gs = slice(g * GROUP_LANES, (g + 1) * GROUP_LANES)
            q_s[:, gs] = proj(COL_Q + g * GROUP_LANES, COL_Q + (g + 1) * GROUP_LANES) * (ATTN_SCALE * LOG2E)
            yield
            for c in range(n_chunks):
                q_cg = q_s[c * chunk:(c + 1) * chunk, gs]
                qrep_s[c, g] = jnp.concatenate(
                    [jnp.where(head_of_lane == h, q_cg, 0.0) for h in range(HEADS_PER_ATTN_GROUP)],
                    axis=0).astype(BF16)
        for col, out_ref, win in ((COL_K, ko_ref, kwin), (COL_V, vo_ref, vwin)):
            for g in range(ATTN_GROUPS):
                gs = slice(g * GROUP_LANES, (g + 1) * GROUP_LANES)
                kv = proj(col + g * GROUP_LANES, col + (g + 1) * GROUP_LANES)
                out_ref[0, :, gs] = kv
                win[WINDOW:WINDOW + tile, gs] = kv.astype(BF16)
                yield
        for g in range(D_SSM // GROUP_LANES):
            gs = slice(g * GROUP_LANES, (g + 1) * GROUP_LANES)
            z_s[:, gs] = proj(COL_Z + g * GROUP_LANES, COL_Z + (g + 1) * GROUP_LANES)
            yield

    def conv_silu():
        base = TAIL_ROWS - conv_hist
        for cblk in range(SSM_CONV_DIM // LANES):
            cs = slice(cblk * LANES, (cblk + 1) * LANES)
            acc = cb_ref[:, cs] + cw_ref[0:1, cs] * xp_s[base:base + tile, cs]
            for tap in range(1, SSM_CONV):
                acc = acc + cw_ref[tap:tap + 1, cs] * xp_s[base + tap:base + tap + tile, cs]
            xc_s[:, cs] = _silu(acc)
            yield

    def to_row_form(blk):
        if chunk < LANES:
            blk = jnp.concatenate([blk, jnp.zeros((LANES - chunk, LANES), F32)], axis=0)
        return blk.T[0:SUBLANES]

    def decay_tables():
        dt = _softplus(dt_raw + dtb_ref[...])
        dt_s[...] = dt
        a = dt * (-jnp.exp(alog_ref[...]))
        pad_rows = ccol_s.shape[0]
        if pad_rows > tile:
            a = jnp.concatenate([a, jnp.zeros((pad_rows - tile, LANES), F32)], axis=0)
        li = lax.broadcasted_iota(jnp.int32, (LANES, LANES), 0)
        si = lax.broadcasted_iota(jnp.int32, (LANES, LANES), 1)
        shift = chunk.bit_length() - 1
        tril_bd = jnp.where((si <= li) & ((li >> shift) == (si >> shift)), 1.0, 0.0).astype(F32)
        yield
        for blk in range(pad_rows // LANES):
            rs = slice(blk * LANES, (blk + 1) * LANES)
            ccol_s[rs] = jnp.dot(tril_bd, a[rs], precision=lax.Precision.HIGHEST,
                                 preferred_element_type=F32)
            yield
        for c in range(n_chunks):
            crow_s[c] = to_row_form(ccol_s[c * chunk:(c + 1) * chunk])
            drow_s[c] = to_row_form(dt_s[c * chunk:(c + 1) * chunk])
            if c % 2 == 1:
                yield

    interleave([attention_projections(), conv_silu(), decay_tables()])

    lane = lax.broadcasted_iota(jnp.int32, (chunk, LANES), 1)
    lo_half = lane < HEAD_DIM
    tri = (lax.broadcasted_iota(jnp.int32, (chunk, chunk), 0)
           >= lax.broadcasted_iota(jnp.int32, (chunk, chunk), 1))
    band_col = lax.broadcasted_iota(jnp.int32, (1, band_pad), 1)
    state_lane = lax.broadcasted_iota(jnp.int32, (1, HEADS_PER_GROUP * SSM_HEAD_DIM), 1)

    def bcast(mat, col, width):
        return jnp.broadcast_to(mat[:, col:col + 1], (chunk, width))

    def pair(mat, h_even):
        return jnp.where(lo_half, bcast(mat, h_even, LANES), bcast(mat, h_even + 1, LANES))

    def row_start(c):
        return c * chunk if isinstance(c, int) else pl.multiple_of(c * chunk, chunk)

    rep_rows = HEADS_PER_ATTN_GROUP * chunk
    group_lane = lax.broadcasted_iota(jnp.int32, (chunk, GROUP_LANES), 1)

    def attn_scores(c, g, s_ref, m_ref):
        r0 = row_start(c)
        q_rep = qrep_s[c, g]
        m_run = None
        for lo in range(0, band_pad, KEY_TILE):
            hi = min(lo + KEY_TILE, band_pad)
            k_t = kwin[pl.ds(pl.multiple_of(r0 + lo, chunk), hi - lo),
                       g * GROUP_LANES:(g + 1) * GROUP_LANES]
            s = _nt_dot(q_rep, k_t) + bias_ref[g, :, lo:hi]
            col = band_col[:, lo:hi]
            if has_past:
                valid = col < band
            else:
                valid = (col < band) & ((t * tile + r0 - WINDOW + col) >= 0)
            s = jnp.where(valid, s, NEG_INF)
            s_ref[:, lo:hi] = s
            for j in range((hi - lo) // LANES):
                s_j = s[:, j * LANES:(j + 1) * LANES]
                m_run = s_j if m_run is None else jnp.maximum(m_run, s_j)
            yield
        m_ref[...] = jnp.broadcast_to(jnp.max(m_run, axis=-1, keepdims=True), (rep_rows, LANES))
        yield

    def attn_softmax(s_ref, m_ref, p_ref, lp_ref):
        for blk in range(rep_rows // SOFTMAX_ROWS):
            rs = slice(blk * SOFTMAX_ROWS, (blk + 1) * SOFTMAX_ROWS)
            m = m_ref[rs, :]
            p_tiles = [jnp.exp2(s_ref[rs, j * LANES:(j + 1) * LANES] - m) for j in range(band_pad // LANES)]
            lp = p_tiles[0]
            for p_j in p_tiles[1:]:
                lp = lp + p_j
            lp_ref[rs, :] = lp
            p_ref[rs, :] = jnp.concatenate(p_tiles, axis=1).astype(BF16)
            yield

    def attn_output(c, g, p_ref, lp_ref):
        r0 = row_start(c)
        inv = 1.0 / jnp.sum(lp_ref[...], axis=-1, keepdims=True)
        yield
        v_g = vwin[pl.ds(r0, band_pad), g * GROUP_LANES:(g + 1) * GROUP_LANES]
        o_all = _dot(p_ref[...], v_g) * inv
        o = o_all[(HEADS_PER_ATTN_GROUP - 1) * chunk:]
        for h in reversed(range(HEADS_PER_ATTN_GROUP - 1)):
            o = jnp.where(group_lane < (h + 1) * HEAD_DIM, o_all[h * chunk:(h + 1) * chunk], o)
        a_s[pl.ds(r0, chunk), g * GROUP_LANES:(g + 1) * GROUP_LANES] = o
        yield

    def ssd_chunk(c):
        r0 = row_start(c)
        rows = pl.ds(r0, chunk)
        dtc = dt_s[rows, :]
        ccol = ccol_s[rows, :]
        crow = crow_s[c]
        drow = drow_s[c]
        clast = ccol[chunk - 1:chunk, :]
        wcol = jnp.exp(clast - ccol) * dtc
        ecol = jnp.exp(ccol)
        cdec = jnp.exp(clast)
        yield
        for g in range(SSM_GROUPS):
            b_g = xc_s[rows, D_SSM + g * SSM_STATE:D_SSM + (g + 1) * SSM_STATE].astype(BF16)
            c_g = xc_s[rows, D_SSM + (SSM_GROUPS + g) * SSM_STATE:
                       D_SSM + (SSM_GROUPS + g + 1) * SSM_STATE].astype(BF16)
            cb = _nt_dot(c_g, b_g)
            h_t = ht_s[g]
            y_off = _dot(c_g, h_t.astype(BF16))
            xw_parts = []
            for jj in range(HEADS_PER_GROUP // 2):
                slab = g * (HEADS_PER_GROUP // 2) + jj
                h_even = HEADS_PER_GROUP * g + 2 * jj
                cs = slice(slab * LANES, (slab + 1) * LANES)
                x2 = xc_s[rows, cs]
                x2_bf = x2.astype(BF16)
                y_diag = None
                for e in range(2):
                    h = h_even + e
                    seg = bcast(ccol, h, chunk) - crow[h:h + 1, 0:chunk]
                    lmat = jnp.where(tri, jnp.exp(seg), 0.0) * drow[h:h + 1, 0:chunk]
                    yd = _dot((cb * lmat).astype(BF16), x2_bf)
                    y_diag = yd if e == 0 else jnp.where(lo_half, y_diag, yd)
                y2 = (y_diag + y_off[:, jj * LANES:(jj + 1) * LANES] * pair(ecol, h_even)
                      + x2 * dsk_ref[:, cs])
                y_s[rows, cs] = y2
                xw_parts.append((x2 * pair(wcol, h_even)).astype(BF16))
                yield
            xw = jnp.concatenate(xw_parts, axis=1)
            dec = None
            for r in reversed(range(HEADS_PER_GROUP)):
                h = HEADS_PER_GROUP * g + r
                d_r = jnp.broadcast_to(cdec[:, h:h + 1], (1, HEADS_PER_GROUP * SSM_HEAD_DIM))
                dec = d_r if dec is None else jnp.where(state_lane < (r + 1) * SSM_HEAD_DIM, d_r, dec)
            ht_s[g] = h_t * dec + _tn_dot(b_g, xw)
            yield

    def step(c, slot, scores_next, softmax_cur, output_prev):
        stages = []
        for g in range(ATTN_GROUPS):
            if scores_next:
                stages.append(attn_scores(c + 1, g, s_bufs[1 - slot][g], m_bufs[1 - slot][g]))
            if softmax_cur:
                stages.append(attn_softmax(s_bufs[slot][g], m_bufs[slot][g], p_bufs[slot][g],
                                           lp_bufs[slot][g]))
            if output_prev:
                stages.append(attn_output(c - 1, g, p_bufs[1 - slot][g], lp_bufs[1 - slot][g]))
        stages.append(ssd_chunk(c))
        interleave(stages)

    interleave(attn_scores(0, g, s_bufs[0][g], m_bufs[0][g]) for g in range(ATTN_GROUPS))
    if n_chunks == 1:
        step(0, 0, False, True, False)
        interleave(attn_output(0, g, p_bufs[0][g], lp_bufs[0][g]) for g in range(ATTN_GROUPS))
    else:
        assert n_chunks % 2 == 0
        step(0, 0, True, True, False)

        for c in range(1, n_chunks - 1):
            step(c, c % 2, True, True, True)
        step(n_chunks - 1, 1, False, True, True)
        interleave(attn_output(n_chunks - 1, g, p_bufs[1][g], lp_bufs[1][g]) for g in range(ATTN_GROUPS))

    for g in range(SSM_GROUPS):
        ho_ref[0, g] = ht_s[g].T

    s_n = _rms_norm(y_s[...] * _silu(z_s[...]), sg_ref[...]).astype(BF16)
    mix_s = _dot(s_n, w_out_ref[D_ATTN:D_ATTN + D_SSM, :])
    a_n = _rms_norm(a_s[...], ag_ref[...]).astype(BF16)
    mix = _dot(a_n, w_out_ref[0:D_ATTN, :]) + mix_s
    x1_ref[0] = _layer_norm(alpha * x + mix, l1g_ref[...], l1b_ref[...])


def _const_spec(shape):
    zeros = (0,) * len(shape)
    return pl.BlockSpec(shape, lambda b, t: zeros, pipeline_mode=pl.Buffered(1))


def _mixer(x, past, weights, *, tile, chunk, alpha):
    bsz, seq, _ = x.shape
    n_tiles = seq // tile
    assert seq % tile == 0 and tile % chunk == 0
    has_past = past is not None
    keep = min(WINDOW, seq)
    assert keep == tile, "the kept K/V rows must be exactly the last tile"
    pad_rows = max(tile, LANES)
    n_chunks = tile // chunk
    rep_rows = HEADS_PER_ATTN_GROUP * chunk
    band = WINDOW + chunk
    band_pad = _round_up(band, LANES)

    def per_stream(shape):
        zeros = (0,) * (len(shape) - 1)
        return pl.BlockSpec((1,) + tuple(shape[1:]), lambda b, t: (b,) + zeros)

    in_specs = [pl.BlockSpec((1, tile, D_MODEL), lambda b, t: (b, t, 0))]
    args = [x]
    if has_past:
        for arr in past:
            in_specs.append(per_stream(arr.shape))
            args.append(arr)
    for w in weights:
        in_specs.append(_const_spec(w.shape))
        args.append(w)

    out_shape = (
        jax.ShapeDtypeStruct((bsz, seq, D_MODEL), F32),
        jax.ShapeDtypeStruct((bsz, keep, D_ATTN), F32),
        jax.ShapeDtypeStruct((bsz, keep, D_ATTN), F32),
        jax.ShapeDtypeStruct((bsz, SSM_GROUPS, HEADS_PER_GROUP * SSM_HEAD_DIM, SSM_STATE), F32),
        jax.ShapeDtypeStruct((bsz, SSM_CONV - 1, SSM_CONV_DIM), F32),
    )
    out_specs = (
        pl.BlockSpec((1, tile, D_MODEL), lambda b, t: (b, t, 0)),
        per_stream(out_shape[1].shape),
        per_stream(out_shape[2].shape),
        per_stream(out_shape[3].shape),
        per_stream(out_shape[4].shape),
    )
    win_rows = WINDOW + tile + band_pad - band
    scratch = [
        pltpu.VMEM((win_rows, D_ATTN), BF16),
        pltpu.VMEM((win_rows, D_ATTN), BF16),
        pltpu.VMEM((tile, D_ATTN), F32),
        pltpu.VMEM((n_chunks, ATTN_GROUPS, rep_rows, GROUP_LANES), BF16),
        pltpu.VMEM((tile, D_SSM), F32),
        pltpu.VMEM((TAIL_ROWS + tile, SSM_CONV_DIM), F32),
        pltpu.VMEM((tile, SSM_CONV_DIM), F32),
        pltpu.VMEM((tile, D_ATTN), F32),
        pltpu.VMEM((tile, D_SSM), F32),
        pltpu.VMEM((tile, LANES), F32),
        pltpu.VMEM((pad_rows, LANES), F32),
        pltpu.VMEM((n_chunks, SUBLANES, LANES), F32),
        pltpu.VMEM((n_chunks, SUBLANES, LANES), F32),
        pltpu.VMEM((SSM_GROUPS, SSM_STATE, HEADS_PER_GROUP * SSM_HEAD_DIM), F32),
    ]
    n_stage = 2 * ATTN_GROUPS
    scratch += [pltpu.VMEM((rep_rows, band_pad), F32) for _ in range(n_stage)]
    scratch += [pltpu.VMEM((rep_rows, LANES), F32) for _ in range(n_stage)]
    scratch += [pltpu.VMEM((rep_rows, band_pad), BF16) for _ in range(n_stage)]
    scratch += [pltpu.VMEM((rep_rows, LANES), F32) for _ in range(n_stage)]
    return pl.pallas_call(
        functools.partial(_mixer_kernel, tile=tile, chunk=chunk, n_tiles=n_tiles, has_past=has_past,
                          alpha=alpha),
        grid=(bsz, n_tiles),
        in_specs=in_specs,
        out_specs=out_specs,
        out_shape=out_shape,
        scratch_shapes=scratch,
        compiler_params=pltpu.CompilerParams(
            dimension_semantics=("arbitrary", "arbitrary"),
            vmem_limit_bytes=VMEM_LIMIT_BYTES),
        name="mixer_past" if has_past else "mixer",
    )(*args)


def _ffn_kernel(*refs, tile, n_streams, has_past, alpha):
    x1_ref = refs[0]
    n_in = 1
    if has_past:
        fp_ref = refs[1]
        n_in = 2
    w_up_ref, fcw_ref, fcb_ref, w_dn_ref, l2g_ref, l2b_ref = refs[n_in:n_in + 6]
    y_ref, fo_ref = refs[n_in + 6:n_in + 8]
    h_s, g_s = refs[n_in + 8:]
    t = pl.program_id(1)
    hist = FFN_CONV - 1
    seg = tile // n_streams
    slot = TAIL_ROWS + seg

    def history(s):
        return slice(s * slot + TAIL_ROWS - hist, s * slot + TAIL_ROWS)

    def newest(s):
        return slice(s * slot + slot - hist, s * slot + slot)

    @pl.when(t == 0)
    def _init():
        for s in range(n_streams):
            h_s[s * slot:s * slot + TAIL_ROWS] = jnp.zeros((TAIL_ROWS, 2 * D_FF), F32)
            if has_past:
                h_s[history(s)] = fp_ref[s]

    @pl.when(t > 0)
    def _carry_history():
        for s in range(n_streams):
            h_s[history(s)] = h_s[newest(s)]

    x1 = x1_ref[0]
    up = _dot(x1.astype(BF16), w_up_ref[...])
    for s in range(n_streams):
        h_s[s * slot + TAIL_ROWS:(s + 1) * slot] = up[s * seg:(s + 1) * seg]
        fo_ref[s] = h_s[newest(s)]

    def conv(cs, s):
        base = s * slot + TAIL_ROWS - hist
        acc = fcb_ref[:, cs] + fcw_ref[0:1, cs] * h_s[base:base + seg, cs]
        for tap in range(1, FFN_CONV):
            acc = acc + fcw_ref[tap:tap + 1, cs] * h_s[base + tap:base + tap + seg, cs]
        return acc

    for cblk in range(D_FF // LANES):
        cs_v = slice(cblk * LANES, (cblk + 1) * LANES)
        cs_g = slice(D_FF + cblk * LANES, D_FF + (cblk + 1) * LANES)
        for s in range(n_streams):
            g_s[s * seg:(s + 1) * seg, cs_v] = (conv(cs_v, s) * _silu(conv(cs_g, s))).astype(BF16)

    f = _dot(g_s[...], w_dn_ref[...])
    y_ref[0] = _layer_norm(alpha * x1 + f, l2g_ref[...], l2b_ref[...])


def _conv_ffn(x1, ffn_past, weights, *, tile, n_streams, alpha):
    bsz, seq, _ = x1.shape
    assert seq % tile == 0 and tile % n_streams == 0
    assert n_streams == 1 or tile == seq
    has_past = ffn_past is not None
    hist_block = (n_streams, FFN_CONV - 1, 2 * D_FF)
    in_specs = [pl.BlockSpec((1, tile, D_MODEL), lambda b, t: (b, t, 0))]
    args = [x1]
    if has_past:
        in_specs.append(pl.BlockSpec(hist_block, lambda b, t: (b, 0, 0)))
        args.append(ffn_past)
    for w in weights:
        in_specs.append(_const_spec(w.shape))
        args.append(w)
    out_shape = (
        jax.ShapeDtypeStruct((bsz, seq, D_MODEL), F32),
        jax.ShapeDtypeStruct((bsz * n_streams, FFN_CONV - 1, 2 * D_FF), F32),
    )
    out_specs = (
        pl.BlockSpec((1, tile, D_MODEL), lambda b, t: (b, t, 0)),
        pl.BlockSpec(hist_block, lambda b, t: (b, 0, 0)),
    )
    scratch = [
        pltpu.VMEM((tile + n_streams * TAIL_ROWS, 2 * D_FF), F32),
        pltpu.VMEM((tile, D_FF), BF16),
    ]
    return pl.pallas_call(
        functools.partial(_ffn_kernel, tile=tile, n_streams=n_streams, has_past=has_past, alpha=alpha),
        grid=(bsz, seq // tile),
        in_specs=in_specs,
        out_specs=out_specs,
        out_shape=out_shape,
        scratch_shapes=scratch,
        compiler_params=pltpu.CompilerParams(
            dimension_semantics=("arbitrary", "arbitrary"),
            vmem_limit_bytes=VMEM_LIMIT_BYTES),
        name="conv_ffn_past" if has_past else "conv_ffn",
    )(*args)


def _pad_lanes(vec):
    return jnp.pad(vec.astype(F32), (0, LANES - vec.shape[0]))[None, :]


def _row(vec):
    return vec.astype(F32)[None, :]


def kernel(x_prompt, x_sample, cache_attn_k, cache_attn_v, state_ssm, state_ssm_conv, state_ffn_conv,
           w_in, rel_bias, attn_norm_g, ssm_conv_w, ssm_conv_b, ssm_dt_bias, ssm_A_log, ssm_D,
           ssm_norm_g, w_out, ln1_g, ln1_b, w_up, ffn_conv_w, ffn_conv_b, w_down, ln2_g, ln2_b):
    depth = w_in.shape[0]
    alpha = (2.0 * depth) ** 0.25
    n_dec, dec_seq, _ = x_sample.shape
    n_cache = cache_attn_k.shape[2]
    assert n_cache == WINDOW

    yp, ys = x_prompt, x_sample
    outs_p, outs_s = [], []
    for l in range(depth):
        w_in_p = jnp.pad(w_in[l], ((0, 0), (0, W_IN_COLS - w_in.shape[2]))).astype(BF16)
        bias = _bias_table(rel_bias[l])

        def grouped_bias(chunk):
            band = WINDOW + chunk
            grouped = bias[:, :chunk, :band].reshape(ATTN_GROUPS, HEADS_PER_ATTN_GROUP * chunk, band)
            return jnp.pad(grouped, ((0, 0), (0, 0), (0, _round_up(band, LANES) - band)))

        mixer_w = (
            w_in_p,
            None,
            _row(attn_norm_g[l]),
            ssm_conv_w[l].astype(F32),
            _row(ssm_conv_b[l]),
            _pad_lanes(ssm_dt_bias[l]),
            _pad_lanes(ssm_A_log[l]),
            _row(jnp.repeat(ssm_D[l], SSM_HEAD_DIM)),
            _row(ssm_norm_g[l]),
            w_out[l].astype(BF16),
            _row(ln1_g[l]),
            _row(ln1_b[l]),
        )
        ffn_w = (
            w_up[l].astype(BF16),
            ffn_conv_w[l].astype(F32),
            _row(ffn_conv_b[l]),
            w_down[l].astype(BF16),
            _row(ln2_g[l]),
            _row(ln2_b[l]),
        )
        w_prompt = (mixer_w[0], grouped_bias(CHUNK)) + mixer_w[2:]
        w_sample = (mixer_w[0], grouped_bias(dec_seq)) + mixer_w[2:]
        x1, kp, vp, hp, cp = _mixer(yp, None, w_prompt, tile=WINDOW, chunk=CHUNK, alpha=alpha)
        yp, fp = _conv_ffn(x1, None, ffn_w, tile=FFN_TILE, n_streams=1, alpha=alpha)
        outs_p.append((kp, vp, hp, cp, fp))
        past = (
            cache_attn_k[l].reshape(n_dec, n_cache, D_ATTN),
            cache_attn_v[l].reshape(n_dec, n_cache, D_ATTN),
            state_ssm[l].reshape(n_dec, SSM_GROUPS, HEADS_PER_GROUP * SSM_HEAD_DIM, SSM_STATE),
            state_ssm_conv[l],
        )
        x1, ks, vs, hs, cs = _mixer(ys, past, w_sample, tile=dec_seq, chunk=dec_seq, alpha=alpha)
        ys, fs = _conv_ffn(x1.reshape(1, n_dec * dec_seq, D_MODEL), state_ffn_conv[l], ffn_w,
                           tile=n_dec * dec_seq, n_streams=n_dec, alpha=alpha)
        ys = ys.reshape(n_dec, dec_seq, D_MODEL)
        outs_s.append((ks, vs, hs, cs, fs))

    def stack(outs, idx, shape_tail):
        return jnp.stack([o[idx].reshape((o[idx].shape[0],) + shape_tail) for o in outs])

    kv_p = (min(WINDOW, x_prompt.shape[1]), N_HEADS, HEAD_DIM)
    kv_s = (dec_seq, N_HEADS, HEAD_DIM)
    st = (SSM_HEADS, SSM_HEAD_DIM, SSM_STATE)
    return (
        yp, ys,
        stack(outs_p, 0, kv_p), stack(outs_p, 1, kv_p),
        stack(outs_s, 0, kv_s), stack(outs_s, 1, kv_s),
        stack(outs_p, 2, st), stack(outs_s, 2, st),
        stack(outs_p, 3, (SSM_CONV - 1, SSM_CONV_DIM)), stack(outs_s, 3, (SSM_CONV - 1, SSM_CONV_DIM)),
        stack(outs_p, 4, (FFN_CONV - 1, 2 * D_FF)), stack(outs_s, 4, (FFN_CONV - 1, 2 * D_FF)),
    )
```
